```python
import math
import jax
import jax.numpy as jnp
from jax import lax
import numpy as np


D_MODEL = 1024
BATCH = 4
SEQ = 8192
DEPTH = 1
DEC_BATCH = 16
DEC_SEQ = 4096
PAST_LEN = 128

D_MIX = D_MODEL
DN_DK = 128
DN_DV = 128
DN_H = D_MIX // (2 * DN_DV)
DN_W = DN_H * DN_DV
DN_CHUNK = 64
SSM_P = 64
SSM_H = D_MIX // (2 * SSM_P)
SSM_W = SSM_H * SSM_P
SSM_G = 2
SSM_N = 128
SSM_CHUNK = 128
CONV_K = 5
D_FF = ((8 * D_MODEL + 3 * 256 - 1) // (3 * 256)) * 256
XBC_W = SSM_W + 2 * SSM_G * SSM_N
IN_SPLITS = (3 * DN_W, DN_W, 2 * DN_H, 2 * DN_H, SSM_W, XBC_W, 2 * SSM_H)
IN_COLS = 3 * DN_W + DN_W + 4 * DN_H + SSM_W + XBC_W + 2 * SSM_H
DEEPNORM_ALPHA = (2 * DEPTH) ** 0.25
DEEPNORM_BETA = (8 * DEPTH) ** -0.25
LN_EPS = 1e-5
RMS_EPS = 1e-6

kernel_name = 'hymba_deltanet_ssd_encoder'


def split_cols(t, sizes):
    out = []
    start = 0
    for s in sizes:
        out.append(t[..., start:start + s])
        start += s
    return out


def layer_norm(x, g, b):
    xf = x.astype(jnp.float32)
    mu = jnp.mean(xf, -1, keepdims=True)
    var = jnp.mean(jnp.square(xf - mu), -1, keepdims=True)
    return ((xf - mu) * lax.rsqrt(var + LN_EPS) * g + b).astype(x.dtype)


def rms_norm(x):
    xf = x.astype(jnp.float32)
    return xf * lax.rsqrt(jnp.mean(jnp.square(xf), -1, keepdims=True) + RMS_EPS)


def l2norm(x):
    return x * lax.rsqrt(jnp.sum(jnp.square(x), -1, keepdims=True) + RMS_EPS)


def centred_dwconv(u, w):
    K = w.shape[0]
    pad = K // 2
    T = u.shape[1]
    up = jnp.pad(u, ((0, 0), (pad, pad), (0, 0)))
    return sum(up[:, i:i + T] * w[i] for i in range(K))


def flip_t(t):
    return jnp.flip(t, axis=1)


def gated_delta_chunked(q, k, v, g, beta):
    f32 = jnp.float32
    Bsz, T, H, Dk = q.shape
    Dv = v.shape[-1]
    C = DN_CHUNK
    N = T // C

    def blocks(t):
        t = t.astype(f32).reshape((Bsz, N, C) + t.shape[2:])
        return jnp.moveaxis(t, 3, 2)

    q = blocks(q) * (Dk ** -0.5)
    k, v, g, beta = blocks(k), blocks(v), blocks(g), blocks(beta)
    g = jnp.cumsum(g, axis=-1)
    incl = jnp.tril(jnp.ones((C, C), bool))
    strict = jnp.tril(jnp.ones((C, C), bool), -1)
    decay = jnp.exp(jnp.where(incl, g[..., :, None] - g[..., None, :], -jnp.inf))
    k_beta = k * beta[..., None]
    L = jnp.where(strict, jnp.einsum('bnhid,bnhjd->bnhij', k_beta, k) * decay, 0.0)
    rhs = jnp.concatenate([k_beta * jnp.exp(g)[..., None], v * beta[..., None]], -1)
    wu = lax.linalg.triangular_solve(L, rhs, left_side=True, lower=True, unit_diagonal=True)
    w, u = wu[..., :Dk], wu[..., Dk:]
    attn = jnp.einsum('bnhid,bnhjd->bnhij', q, k) * decay
    q_dec = q * jnp.exp(g)[..., None]
    g_last = g[..., -1]
    k_dec = k * jnp.exp(g_last[..., None] - g)[..., None]

    def step(S, xs):
        q_c, k_c, w_c, u_c, a_c, gl_c = xs
        v_new = u_c - jnp.einsum('bhcd,bhde->bhce', w_c, S)
        o = jnp.einsum('bhcd,bhde->bhce', q_c, S) + jnp.einsum('bhij,bhje->bhie', a_c, v_new)
        S = S * jnp.exp(gl_c)[..., None, None] + jnp.einsum('bhcd,bhce->bhde', k_c, v_new)
        return S, o

    xs = tuple(jnp.moveaxis(t, 1, 0) for t in (q_dec, k_dec, w, u, attn, g_last))
    S0 = jnp.zeros((Bsz, H, Dk, Dv), f32)
    _, o = lax.scan(step, S0, xs)
    o = jnp.moveaxis(o, 0, 1)
    return jnp.moveaxis(o, 2, 3).reshape(Bsz, T, H, Dv)


def ssd_chunked(x, dt, A, Bm, Cm):
    f32 = jnp.float32
    Bsz, T, H, P = x.shape
    G, N = Bm.shape[2], Bm.shape[3]
    R = H // G
    Lc = SSM_CHUNK
    Nc = T // Lc
    x = x.astype(f32).reshape(Bsz, Nc, Lc, G, R, P)
    dt = dt.astype(f32).reshape(Bsz, Nc, Lc, G, R)
    Bm = Bm.astype(f32).reshape(Bsz, Nc, Lc, G, N)
    Cm = Cm.astype(f32).reshape(Bsz, Nc, Lc, G, N)
    a = jnp.transpose(dt * A.reshape(G, R), (0, 3, 4, 1, 2))
    a_cs = jnp.cumsum(a, -1)
    xdt = x * dt[..., None]
    incl = jnp.tril(jnp.ones((Lc, Lc), bool))
    Lmat = jnp.exp(jnp.where(incl, a_cs[..., :, None] - a_cs[..., None, :], -jnp.inf))
    CB = jnp.einsum('bclgn,bcsgn->bgcls', Cm, Bm)
    y_diag = jnp.einsum('bgrcls,bcsgrp->bclgrp', CB[:, :, None] * Lmat, xdt)
    decay_states = jnp.exp(a_cs[..., -1:] - a_cs)
    states = jnp.einsum('bclgn,bgrcl,bclgrp->cbgrpn', Bm, decay_states, xdt)
    chunk_decay = jnp.moveaxis(jnp.exp(a_cs[..., -1]), 3, 0)

    def step(S, inp):
        st, dec = inp
        return S * dec[..., None, None] + st, S

    S0 = jnp.zeros((Bsz, G, R, P, N), f32)
    _, S_prev = lax.scan(step, S0, (states, chunk_decay))
    y_off = jnp.einsum('bclgn,cbgrpn,bgrcl->bclgrp', Cm, S_prev, jnp.exp(a_cs))
    return (y_diag + y_off).reshape(Bsz, T, H, P)


def gated_deltanet_mixer(qkv, z, a, b, conv_w, A_log, dt_bias, norm_w):
    f32 = jnp.float32
    Bsz, T, _ = qkv.shape
    qkv = jax.nn.silu(centred_dwconv(qkv.astype(f32), conv_w.astype(f32)))
    q, k, v = jnp.split(qkv, 3, axis=-1)
    q = l2norm(q.reshape(Bsz, T, DN_H, DN_DK))
    k = l2norm(k.reshape(Bsz, T, DN_H, DN_DK))
    v = v.reshape(Bsz, T, DN_H, DN_DV)
    a = a.astype(f32).reshape(Bsz, T, 2, DN_H)
    b = b.astype(f32).reshape(Bsz, T, 2, DN_H)
    g = -jnp.exp(A_log.astype(f32)) * jax.nn.softplus(a + dt_bias.astype(f32))
    beta = jax.nn.sigmoid(b)
    o_f = gated_delta_chunked(q, k, v, g[:, :, 0], beta[:, :, 0])
    o_b = flip_t(gated_delta_chunked(flip_t(q), flip_t(k), flip_t(v),
                                     flip_t(g[:, :, 1]), flip_t(beta[:, :, 1])))
    zf = z.astype(f32).reshape(Bsz, T, DN_H, DN_DV)
    o = rms_norm(o_f + o_b) * norm_w * jax.nn.silu(zf)
    return o.reshape(Bsz, T, DN_W).astype(z.dtype)


def mamba2_ssd_mixer(z, xbc, dt_raw, conv_w, conv_b, A_log, dt_bias, D, norm_w):
    f32 = jnp.float32
    Bsz, T, _ = xbc.shape
    xbc = jax.nn.silu(centred_dwconv(xbc.astype(f32), conv_w.astype(f32)) + conv_b)
    xs, Bm, Cm = split_cols(xbc, (SSM_W, SSM_G * SSM_N, SSM_G * SSM_N))
    xs = xs.reshape(Bsz, T, SSM_H, SSM_P)
    Bm = Bm.reshape(Bsz, T, SSM_G, SSM_N)
    Cm = Cm.reshape(Bsz, T, SSM_G, SSM_N)
    dt = jax.nn.softplus(dt_raw.astype(f32).reshape(Bsz, T, 2, SSM_H) + dt_bias.astype(f32))
    A = -jnp.exp(A_log.astype(f32))
    y_f = ssd_chunked(xs, dt[:, :, 0], A[0], Bm, Cm)
    y_b = flip_t(ssd_chunked(flip_t(xs), flip_t(dt[:, :, 1]), A[1], flip_t(Bm), flip_t(Cm)))
    y = y_f + y_b + D[:, None] * xs
    y = y.reshape(Bsz, T, SSM_G, SSM_W // SSM_G) * jax.nn.silu(
        z.astype(f32).reshape(Bsz, T, SSM_G, SSM_W // SSM_G))
    y = rms_norm(y) * norm_w.reshape(SSM_G, SSM_W // SSM_G)
    return y.reshape(Bsz, T, SSM_W).astype(z.dtype)


def encoder_layer(x, c, w_ada, b_ada, w_in, dn_conv_w, dn_A_log, dn_dt_bias, dn_norm_w,
                  ssm_conv_w, ssm_conv_b, ssm_A_log, ssm_dt_bias, ssm_D, ssm_norm_w,
                  w_out, ln1_g, ln1_b, w_gate, w_up, w_down, ln2_g, ln2_b):
    mod = jax.nn.silu(c) @ w_ada + b_ada
    sh1, sc1, gt1, sh2, sc2, gt2 = jnp.split(mod[:, None, :], 6, axis=-1)
    h = x * (1 + sc1) + sh1
    proj = h @ w_in
    dn_qkv, dn_z, dn_a, dn_b, ssm_z, ssm_xbc, ssm_dt = split_cols(proj, IN_SPLITS)
    o_dn = gated_deltanet_mixer(dn_qkv, dn_z, dn_a, dn_b, dn_conv_w, dn_A_log, dn_dt_bias, dn_norm_w)
    o_ssm = mamba2_ssd_mixer(ssm_z, ssm_xbc, ssm_dt, ssm_conv_w, ssm_conv_b, ssm_A_log,
                             ssm_dt_bias, ssm_D, ssm_norm_w)
    mix = jnp.concatenate([o_dn, o_ssm], -1) @ w_out
    x = layer_norm(DEEPNORM_ALPHA * x + gt1 * mix, ln1_g, ln1_b)
    h = x * (1 + sc2) + sh2
    ffn = (jax.nn.silu(h @ w_gate) * (h @ w_up)) @ w_down
    return layer_norm(DEEPNORM_ALPHA * x + gt2 * ffn, ln2_g, ln2_b)


def setup_inputs(seed: int = 0) -> dict:
    key = jax.random.key(seed)
    ks = jax.random.split(key, 32)
    f32 = jnp.float32

    def nrm(k, shape, s):
        return jax.random.normal(k, shape, f32) * s

    def dt_bias_init(k, shape):
        dt0 = jnp.exp(jax.random.uniform(k, shape, f32, math.log(1e-3), math.log(1e-1)))
        return dt0 + jnp.log(-jnp.expm1(-dt0))

    def a_log_init(k, shape):
        return jnp.log(jax.random.uniform(k, shape, f32, 1.0, 16.0))

    return {
        'x_prompt': nrm(ks[0], (BATCH, SEQ, D_MODEL), 1.0),
        'x_sample': nrm(ks[1], (DEC_BATCH, DEC_SEQ, D_MODEL), 1.0),
        'c_prompt': nrm(ks[2], (BATCH, D_MODEL), 1.0),
        'c_sample': nrm(ks[3], (DEC_BATCH, D_MODEL), 1.0),
        'w_ada': nrm(ks[4], (DEPTH, D_MODEL, 6 * D_MODEL), D_MODEL ** -0.5),
        'b_ada': nrm(ks[5], (DEPTH, 6 * D_MODEL), 0.02),
        'w_in': nrm(ks[6], (DEPTH, D_MODEL, IN_COLS), D_MODEL ** -0.5),
        'dn_conv_w': nrm(ks[7], (DEPTH, CONV_K, 3 * DN_W), CONV_K ** -0.5),
        'dn_A_log': a_log_init(ks[8], (DEPTH, 2, DN_H)),
        'dn_dt_bias': dt_bias_init(ks[9], (DEPTH, 2, DN_H)),
        'dn_norm_w': 1.0 + nrm(ks[10], (DEPTH, DN_DV), 0.02),
        'ssm_conv_w': nrm(ks[11], (DEPTH, CONV_K, XBC_W), CONV_K ** -0.5),
        'ssm_conv_b': nrm(ks[12], (DEPTH, XBC_W), 0.02),
        'ssm_A_log': a_log_init(ks[13], (DEPTH, 2, SSM_H)),
        'ssm_dt_bias': dt_bias_init(ks[14], (DEPTH, 2, SSM_H)),
        'ssm_D': 1.0 + nrm(ks[15], (DEPTH, SSM_H), 0.02),
        'ssm_norm_w': 1.0 + nrm(ks[16], (DEPTH, SSM_W), 0.02),
        'w_out': nrm(ks[17], (DEPTH, D_MIX, D_MODEL), DEEPNORM_BETA * D_MIX ** -0.5),
        'ln1_g': 1.0 + nrm(ks[18], (DEPTH, D_MODEL), 0.02),
        'ln1_b': nrm(ks[19], (DEPTH, D_MODEL), 0.02),
        'w_gate': nrm(ks[20], (DEPTH, D_MODEL, D_FF), D_MODEL ** -0.5),
        'w_up': nrm(ks[21], (DEPTH, D_MODEL, D_FF), D_MODEL ** -0.5),
        'w_down': nrm(ks[22], (DEPTH, D_FF, D_MODEL), DEEPNORM_BETA * D_FF ** -0.5),
        'ln2_g': 1.0 + nrm(ks[23], (DEPTH, D_MODEL), 0.02),
        'ln2_b': nrm(ks[24], (DEPTH, D_MODEL), 0.02),
    }


def reference(x_prompt, x_sample, c_prompt, c_sample, w_ada, b_ada, w_in, dn_conv_w, dn_A_log,
              dn_dt_bias, dn_norm_w, ssm_conv_w, ssm_conv_b, ssm_A_log, ssm_dt_bias, ssm_D,
              ssm_norm_w, w_out, ln1_g, ln1_b, w_gate, w_up, w_down, ln2_g, ln2_b):
    params = (w_ada, b_ada, w_in, dn_conv_w, dn_A_log, dn_dt_bias, dn_norm_w, ssm_conv_w,
              ssm_conv_b, ssm_A_log, ssm_dt_bias, ssm_D, ssm_norm_w, w_out, ln1_g, ln1_b,
              w_gate, w_up, w_down, ln2_g, ln2_b)

    def run(x, c):
        for l in range(DEPTH):
            x = encoder_layer(x, c, *[p[l] for p in params])
        return x

    y_prompt = run(x_prompt, c_prompt)
    y_sample = run(x_sample, c_sample)
    return (y_prompt, y_sample)
```

```python
import functools
import math

import jax
import jax.numpy as jnp
from jax import lax
from jax.experimental import pallas as pl
from jax.experimental.pallas import tpu as pltpu

F32 = jnp.float32
BF16 = jnp.bfloat16
HIGHEST = lax.Precision.HIGHEST

D_MODEL = 1024
DN_H = 4
DN_DK = 128
DN_W = DN_H * DN_DK
SSM_H = 8
SSM_P = 64
SSM_W = SSM_H * SSM_P
SSM_G = 2
SSM_N = 128
XBC_W = SSM_W + 2 * SSM_G * SSM_N
CONV_K = 5
D_FF = 2816
CONV_W = 3 * DN_W + XBC_W
LN_EPS = 1e-5
RMS_EPS = 1e-6

CH = 128
HALO = 16
LANES = 128

G_GC = 0
G_BETA = 8
G_DT = 16
G_ACS = 32

VMEM_LIMIT = 56 * 1024 * 1024


def _sigmoid(v):
    return 1.0 / (1.0 + jnp.exp(-v))


def _silu(v):
    return v * _sigmoid(v)


def _softplus(v):
    return jnp.maximum(v, 0.0) + jnp.log1p(jnp.exp(-jnp.abs(v)))


def _dot(a, b):
    return jnp.dot(a, b, preferred_element_type=F32)


def _dot_nt(a, b):
    return lax.dot_general(a, b, (((1,), (1,)), ((), ())), preferred_element_type=F32)


def _dot_tn(a, b):
    return lax.dot_general(a, b, (((0,), (0,)), ((), ())), preferred_element_type=F32)


def _const_spec(shape):
    nd = len(shape)
    return pl.BlockSpec(shape, lambda *_: (0,) * nd, pipeline_mode=pl.Buffered(1))


def _mod_kernel(c_ref, w_ref, b_ref, o_ref):
    s = _silu(c_ref[...])
    o_ref[...] = jnp.dot(s, w_ref[...], precision=HIGHEST, preferred_element_type=F32) + b_ref[...]


def _mod_call(c_pad, w_ada, b_ada):
    bp = c_pad.shape[0]
    n = w_ada.shape[1]
    blk = 1024
    return pl.pallas_call(
        _mod_kernel,
        grid=(n // blk,),
        in_specs=[pl.BlockSpec((bp, D_MODEL), lambda j: (0, 0)),
                  pl.BlockSpec((D_MODEL, blk), lambda j: (0, j)),
                  pl.BlockSpec((1, blk), lambda j: (0, j))],
        out_specs=pl.BlockSpec((bp, blk), lambda j: (0, j)),
        out_shape=jax.ShapeDtypeStruct((bp, n), F32),
        compiler_params=pltpu.CompilerParams(dimension_semantics=("arbitrary",), vmem_limit_bytes=VMEM_LIMIT),
        name="mod",
    )(c_pad, w_ada, b_ada)


def _inproj_kernel(xm_ref, xp_ref, xn_ref, mod_ref, wbig_ref, wz_ref, wsm_ref, cw_ref, cb_ref, brow_ref, arow_ref,
                   qkv_ref, xbc_ref, zz_ref, g_ref, gt_ref, pbuf, *, tb, nt):
    i = pl.program_id(1)
    sh = mod_ref[0, 0:1, :]
    sc = mod_ref[0, 1:2, :]

    def modulate(v):
        return v * (1.0 + sc) + sh

    hm = modulate(xm_ref[0])
    hm16 = hm.astype(BF16)
    h_all = jnp.concatenate([modulate(xp_ref[0]).astype(BF16), hm16, modulate(xn_ref[0]).astype(BF16)], axis=0)
    pbuf[...] = _dot(h_all, wbig_ref[...])

    @pl.when(i == 0)
    def _():
        pbuf[0:HALO, :] = jnp.zeros((HALO, CONV_W), F32)

    @pl.when(i == nt - 1)
    def _():
        pbuf[HALO + tb:2 * HALO + tb, :] = jnp.zeros((HALO, CONV_W), F32)

    rb = 64
    base = HALO - CONV_K // 2
    for cg in range(CONV_W // LANES):
        cs = slice(cg * LANES, (cg + 1) * LANES)
        w = [cw_ref[j:j + 1, cs] for j in range(CONV_K)]
        bias = cb_ref[0:1, cs]
        for r0 in range(0, tb, rb):
            acc = bias + pbuf[base + r0:base + r0 + rb, cs] * w[0]
            for j in range(1, CONV_K):
                acc = acc + pbuf[base + j + r0:base + j + r0 + rb, cs] * w[j]
            y = _silu(acc)
            if cg < 2 * DN_H:
                y = y * lax.rsqrt(jnp.sum(y * y, axis=-1, keepdims=True) + RMS_EPS)
                if cg < DN_H:
                    y = y * (DN_DK ** -0.5)
            if cg < 3 * DN_H:
                qkv_ref[0, r0:r0 + rb, cs] = y
            else:
                c2 = cg - 3 * DN_H
                xbc_ref[0, r0:r0 + rb, c2 * LANES:(c2 + 1) * LANES] = y

    zz_ref[0] = _dot(hm16, wz_ref[...])

    ps = jnp.dot(hm, wsm_ref[...], precision=HIGHEST, preferred_element_type=F32)
    lane = lax.broadcasted_iota(jnp.int32, (1, LANES), 1)
    sp = _softplus(ps + brow_ref[...])
    decay_src = sp * arow_ref[...]
    is_beta = (lane >= G_BETA) & (lane < G_DT)
    is_dt = (lane >= G_DT) & (lane < G_ACS)
    raw = jnp.where(is_beta, _sigmoid(ps), jnp.where(is_dt, sp, 0.0))
    r = lax.broadcasted_iota(jnp.int32, (tb, tb), 0)
    c = lax.broadcasted_iota(jnp.int32, (tb, tb), 1)
    same = (r // CH) == (c // CH)
    tril = (same & (r >= c)).astype(F32)
    triu = (same & (r <= c)).astype(F32)
    cs_f = jnp.dot(tril, decay_src, precision=HIGHEST, preferred_element_type=F32)
    cs_b = jnp.dot(triu, decay_src, precision=HIGHEST, preferred_element_type=F32)
    fwd_cum = (lane < G_GC + DN_H) | ((lane >= G_ACS) & (lane < G_ACS + SSM_H))
    bwd_cum = ((lane >= G_GC + DN_H) & (lane < G_BETA)) | ((lane >= G_ACS + SSM_H) & (lane < G_ACS + 2 * SSM_H))
    gates = jnp.where(fwd_cum, cs_f, jnp.where(bwd_cum, cs_b, raw))
    g_ref[0] = gates
    gt_ref[0] = gates.T


def _inproj_call(x, mod, wbig, wz, wsm, cw, cb, brow, arow, tb):
    bsz, t, _ = x.shape
    nt = t // tb
    hb = tb // HALO
    nhb = t // HALO
    kern = functools.partial(_inproj_kernel, tb=tb, nt=nt)
    return pl.pallas_call(
        kern,
        grid=(bsz, nt),
        in_specs=[
            pl.BlockSpec((1, tb, D_MODEL), lambda b, i: (b, i, 0)),
            pl.BlockSpec((1, HALO, D_MODEL), lambda b, i: (b, jnp.maximum(i * hb - 1, 0), 0)),
            pl.BlockSpec((1, HALO, D_MODEL), lambda b, i: (b, jnp.minimum((i + 1) * hb, nhb - 1), 0)),
            pl.BlockSpec((1, 6, D_MODEL), lambda b, i: (b, 0, 0)),
            _const_spec((D_MODEL, CONV_W)),
            _const_spec((D_MODEL, 2 * DN_W)),
            _const_spec((D_MODEL, LANES)),
            _const_spec((CONV_K, CONV_W)),
            _const_spec((1, CONV_W)),
            _const_spec((1, LANES)),
            _const_spec((1, LANES)),
        ],
        out_specs=[
            pl.BlockSpec((1, tb, 3 * DN_W), lambda b, i: (b, i, 0)),
            pl.BlockSpec((1, tb, XBC_W), lambda b, i: (b, i, 0)),
            pl.BlockSpec((1, tb, 2 * DN_W), lambda b, i: (b, i, 0)),
            pl.BlockSpec((1, tb, LANES), lambda b, i: (b, i, 0)),
            pl.BlockSpec((1, LANES, tb), lambda b, i: (b, 0, i)),
        ],
        out_shape=[
            jax.ShapeDtypeStruct((bsz, t, 3 * DN_W), F32),
            jax.ShapeDtypeStruct((bsz, t, XBC_W), F32),
            jax.ShapeDtypeStruct((bsz, t, 2 * DN_W), F32),
            jax.ShapeDtypeStruct((bsz, t, LANES), F32),
            jax.ShapeDtypeStruct((bsz, LANES, t), F32),
        ],
        scratch_shapes=[pltpu.VMEM((tb + 2 * HALO, CONV_W), F32)],
        compiler_params=pltpu.CompilerParams(dimension_semantics=("arbitrary", "arbitrary"),
                                             vmem_limit_bytes=VMEM_LIMIT),
        name="inproj",
    )(x, x, x, mod, wbig, wz, wsm, cw, cb, brow, arow)


def _dn_kernel(*refs, rev, final):
    if final:
        qkv_ref, g_ref, gt_ref, of_ref, z_ref, nw_ref, o_ref, s_ref = refs
    else:
        qkv_ref, g_ref, gt_ref, o_ref, s_ref = refs

    @pl.when(pl.program_id(1) == 0)
    def _():
        s_ref[...] = jnp.zeros_like(s_ref)

    r = lax.broadcasted_iota(jnp.int32, (CH, CH), 0)
    c = lax.broadcasted_iota(jnp.int32, (CH, CH), 1)
    incl = (r <= c) if rev else (r >= c)
    strict = (r < c) if rev else (r > c)
    eye = (r == c).astype(F32)
    lo = DN_H if rev else 0
    last = 0 if rev else CH - 1
    gcols = g_ref[0]
    grows = gt_ref[0]
    for h in range(DN_H):
        hs = slice(h * DN_DK, (h + 1) * DN_DK)
        q = qkv_ref[0, :, hs]
        k = qkv_ref[0, :, DN_W + h * DN_DK:DN_W + (h + 1) * DN_DK]
        v = qkv_ref[0, :, 2 * DN_W + h * DN_DK:2 * DN_W + (h + 1) * DN_DK]
        gi = G_GC + lo + h
        gcc = gcols[:, gi:gi + 1]
        gcr = grows[gi:gi + 1, :]
        beta = gcols[:, G_BETA + lo + h:G_BETA + lo + h + 1]
        glast = gcols[last:last + 1, gi:gi + 1]
        decay = jnp.exp(jnp.where(incl, gcc - gcr, -jnp.inf))
        k16 = k.astype(BF16)
        lmat = jnp.where(strict, _dot_nt((k * beta).astype(BF16), k16) * decay, 0.0)
        xinv = eye - jnp.where((r >> 1) == (c >> 1), lmat, 0.0)
        for s in range(1, 7):
            off = ((r >> (s + 1)) == (c >> (s + 1))) & ((r >> s) != (c >> s))
            x16 = xinv.astype(BF16)
            t1 = _dot(jnp.where(off, lmat, 0.0).astype(BF16), x16)
            xinv = xinv - _dot(x16, t1.astype(BF16))
        attn = _dot_nt(q.astype(BF16), k16) * decay
        eg = jnp.exp(gcc)
        st = s_ref[h]
        st16 = st.astype(BF16)
        resid = v - _dot((k * eg).astype(BF16), st16)
        vnew16 = _dot(xinv.astype(BF16), (beta * resid).astype(BF16)).astype(BF16)
        o = _dot((q * eg).astype(BF16), st16) + _dot(attn.astype(BF16), vnew16)
        kdec = k * jnp.exp(glast - gcc)
        s_ref[h] = st * jnp.exp(glast) + _dot_tn(kdec.astype(BF16), vnew16)
        if final:
            o = o + of_ref[0, :, hs]
            o = o * lax.rsqrt(jnp.mean(o * o, axis=-1, keepdims=True) + RMS_EPS)
            o = o * nw_ref[...] * _silu(z_ref[0, :, hs])
        o_ref[0, :, hs] = o


def _dn_call(qkv, gates, gates_t, rev, o_fwd=None, zz=None, norm_w=None):
    bsz, t, _ = qkv.shape
    nc = t // CH
    final = o_fwd is not None

    def cidx(i):
        return (nc - 1 - i) if rev else i

    in_specs = [
        pl.BlockSpec((1, CH, 3 * DN_W), lambda b, i: (b, cidx(i), 0)),
        pl.BlockSpec((1, CH, LANES), lambda b, i: (b, cidx(i), 0)),
        pl.BlockSpec((1, LANES, CH), lambda b, i: (b, 0, cidx(i))),
    ]
    args = [qkv, gates, gates_t]
    if final:
        in_specs += [
            pl.BlockSpec((1, CH, DN_W), lambda b, i: (b, cidx(i), 0)),
            pl.BlockSpec((1, CH, DN_W), lambda b, i: (b, cidx(i), 0)),
            _const_spec((1, DN_DK)),
        ]
        args += [o_fwd, zz, norm_w]
    return pl.pallas_call(
        functools.partial(_dn_kernel, rev=rev, final=final),
        grid=(bsz, nc),
        in_specs=in_specs,
        out_specs=pl.BlockSpec((1, CH, DN_W), lambda b, i: (b, cidx(i), 0)),
        out_shape=jax.ShapeDtypeStruct((bsz, t, DN_W), F32),
        scratch_shapes=[pltpu.VMEM((DN_H, DN_DK, DN_DK), F32)],
        compiler_params=pltpu.CompilerParams(dimension_semantics=("arbitrary", "arbitrary"),
                                             vmem_limit_bytes=VMEM_LIMIT),
        name="dn_bwd" if rev else "dn_fwd",
    )(*args)


def _ssd_kernel(*refs, rev, final):
    if final:
        xbc_ref, g_ref, gt_ref, yf_ref, z_ref, d_ref, nw_ref, y_ref, s_ref = refs
    else:
        xbc_ref, g_ref, gt_ref, y_ref, s_ref = refs

    @pl.when(pl.program_id(1) == 0)
    def _():
        s_ref[...] = jnp.zeros_like(s_ref)

    r = lax.broadcasted_iota(jnp.int32, (CH, CH), 0)
    c = lax.broadcasted_iota(jnp.int32, (CH, CH), 1)
    incl = (r <= c) if rev else (r >= c)
    left = lax.broadcasted_iota(jnp.int32, (1, LANES), 1) < SSM_P
    lo = SSM_H if rev else 0
    last = 0 if rev else CH - 1
    gcols = g_ref[0]
    grows = gt_ref[0]
    heads_per_group = SSM_H // SSM_G
    for g in range(SSM_G):
        b16 = xbc_ref[0, :, SSM_W + g * SSM_N:SSM_W + (g + 1) * SSM_N].astype(BF16)
        c16 = xbc_ref[0, :, SSM_W + SSM_G * SSM_N + g * SSM_N:SSM_W + SSM_G * SSM_N + (g + 1) * SSM_N].astype(BF16)
        cb = _dot_nt(c16, b16)
        ys = []
        for pp in range(heads_per_group // 2):
            p = g * (heads_per_group // 2) + pp
            ps = slice(p * LANES, (p + 1) * LANES)
            h0, h1 = 2 * p, 2 * p + 1
            x = xbc_ref[0, :, ps]
            d0 = G_DT + lo + h0
            a0 = G_ACS + lo + h0
            dt = jnp.where(left, gcols[:, d0:d0 + 1], gcols[:, d0 + 1:d0 + 2])
            ac0, ac1 = gcols[:, a0:a0 + 1], gcols[:, a0 + 1:a0 + 2]
            ar0, ar1 = grows[a0:a0 + 1, :], grows[a0 + 1:a0 + 2, :]
            l0 = jnp.exp(jnp.where(incl, ac0 - ar0, -jnp.inf))
            l1 = jnp.exp(jnp.where(incl, ac1 - ar1, -jnp.inf))
            xdt = x * dt
            y = (_dot((cb * l0).astype(BF16), jnp.where(left, xdt, 0.0).astype(BF16))
                 + _dot((cb * l1).astype(BF16), jnp.where(left, 0.0, xdt).astype(BF16)))
            acp = jnp.where(left, ac0, ac1)
            alp = jnp.where(left, gcols[last:last + 1, a0:a0 + 1], gcols[last:last + 1, a0 + 1:a0 + 2])
            st = s_ref[p]
            y = y + _dot(c16, st.astype(BF16)) * jnp.exp(acp)
            s_ref[p] = st * jnp.exp(alp) + _dot_tn(b16, (xdt * jnp.exp(alp - acp)).astype(BF16))
            if final:
                y = y + yf_ref[0, :, ps] + d_ref[0:1, ps] * x
                y = y * _silu(z_ref[0, :, ps])
            ys.append((ps, y))
        if final:
            ssq = sum(jnp.sum(y * y, axis=-1, keepdims=True) for _, y in ys)
            inv = lax.rsqrt(ssq / (SSM_W // SSM_G) + RMS_EPS)
            for ps, y in ys:
                y_ref[0, :, ps] = y * inv * nw_ref[0:1, ps]
        else:
            for ps, y in ys:
                y_ref[0, :, ps] = y


def _ssd_call(xbc, gates, gates_t, rev, y_fwd=None, zz=None, d_row=None, norm_w=None):
    bsz, t, _ = xbc.shape
    nc = t // CH
    final = y_fwd is not None

    def cidx(i):
        return (nc - 1 - i) if rev else i

    in_specs = [
        pl.BlockSpec((1, CH, XBC_W), lambda b, i: (b, cidx(i), 0)),
        pl.BlockSpec((1, CH, LANES), lambda b, i: (b, cidx(i), 0)),
        pl.BlockSpec((1, LANES, CH), lambda b, i: (b, 0, cidx(i))),
    ]
    args = [xbc, gates, gates_t]
    if final:
        in_specs += [
            pl.BlockSpec((1, CH, SSM_W), lambda b, i: (b, cidx(i), 0)),
            pl.BlockSpec((1, CH, SSM_W), lambda b, i: (b, cidx(i), 1)),
            _const_spec((1, SSM_W)),
            _const_spec((1, SSM_W)),
        ]
        args += [y_fwd, zz, d_row, norm_w]
    return pl.pallas_call(
        functools.partial(_ssd_kernel, rev=rev, final=final),
        grid=(bsz, nc),
        in_specs=in_specs,
        out_specs=pl.BlockSpec((1, CH, SSM_W), lambda b, i: (b, cidx(i), 0)),
        out_shape=jax.ShapeDtypeStruct((bsz, t, SSM_W), F32),
        scratch_shapes=[pltpu.VMEM((SSM_H // 2, SSM_N, 2 * SSM_P), F32)],
        compiler_params=pltpu.CompilerParams(dimension_semantics=("arbitrary", "arbitrary"),
                                             vmem_limit_bytes=VMEM_LIMIT),
        name="ssd_bwd" if rev else "ssd_fwd",
    )(*args)


def _layer_norm(v, g, b):
    mu = jnp.mean(v, axis=-1, keepdims=True)
    d = v - mu
    var = jnp.mean(d * d, axis=-1, keepdims=True)
    return d * lax.rsqrt(var + LN_EPS) * g + b


def _tail_kernel(odn_ref, ossm_ref, x_ref, mod_ref, wout_ref, l1g_ref, l1b_ref, wg_ref, wu_ref, wd_ref,
                 l2g_ref, l2b_ref, out_ref, *, alpha):
    gt1 = mod_ref[0, 2:3, :]
    sh2 = mod_ref[0, 3:4, :]
    sc2 = mod_ref[0, 4:5, :]
    gt2 = mod_ref[0, 5:6, :]
    mix = (_dot(odn_ref[0].astype(BF16), wout_ref[0:DN_W, :])
           + _dot(ossm_ref[0].astype(BF16), wout_ref[DN_W:DN_W + SSM_W, :]))
    x1 = _layer_norm(alpha * x_ref[0] + gt1 * mix, l1g_ref[...], l1b_ref[...])
    h16 = (x1 * (1.0 + sc2) + sh2).astype(BF16)
    act = (_silu(_dot(h16, wg_ref[...])) * _dot(h16, wu_ref[...])).astype(BF16)
    ffn = _dot(act, wd_ref[...])
    out_ref[0] = _layer_norm(alpha * x1 + gt2 * ffn, l2g_ref[...], l2b_ref[...])


def _tail_call(o_dn, o_ssm, x, mod, wout, l1g, l1b, wg, wu, wd, l2g, l2b, tb, alpha):
    bsz, t, _ = x.shape
    nt = t // tb
    return pl.pallas_call(
        functools.partial(_tail_kernel, alpha=alpha),
        grid=(bsz, nt),
        in_specs=[
            pl.BlockSpec((1, tb, DN_W), lambda b, i: (b, i, 0)),
            pl.BlockSpec((1, tb, SSM_W), lambda b, i: (b, i, 0)),
            pl.BlockSpec((1, tb, D_MODEL), lambda b, i: (b, i, 0)),
            pl.BlockSpec((1, 6, D_MODEL), lambda b, i: (b, 0, 0)),
            _const_spec((DN_W + SSM_W, D_MODEL)),
            _const_spec((1, D_MODEL)),
            _const_spec((1, D_MODEL)),
            _const_spec((D_MODEL, D_FF)),
            _const_spec((D_MODEL, D_FF)),
            _const_spec((D_FF, D_MODEL)),
            _const_spec((1, D_MODEL)),
            _const_spec((1, D_MODEL)),
        ],
        out_specs=pl.BlockSpec((1, tb, D_MODEL), lambda b, i: (b, i, 0)),
        out_shape=jax.ShapeDtypeStruct((bsz, t, D_MODEL), F32),
        compiler_params=pltpu.CompilerParams(dimension_semantics=("arbitrary", "arbitrary"),
                                             vmem_limit_bytes=VMEM_LIMIT),
        name="tail",
    )(o_dn, o_ssm, x, mod, wout, l1g, l1b, wg, wu, wd, l2g, l2b)


def _prep_layer(w_in, dn_conv_w, dn_A_log, dn_dt_bias, dn_norm_w, ssm_conv_w, ssm_conv_b, ssm_A_log,
                ssm_dt_bias, ssm_D, ssm_norm_w, w_out, ln1_g, ln1_b, w_gate, w_up, w_down, ln2_g, ln2_b):
    o = 0
    w_qkv = w_in[:, o:o + 3 * DN_W]; o += 3 * DN_W
    w_dz = w_in[:, o:o + DN_W]; o += DN_W
    w_a = w_in[:, o:o + 2 * DN_H]; o += 2 * DN_H
    w_b = w_in[:, o:o + 2 * DN_H]; o += 2 * DN_H
    w_sz = w_in[:, o:o + SSM_W]; o += SSM_W
    w_xbc = w_in[:, o:o + XBC_W]; o += XBC_W
    w_dt = w_in[:, o:o + 2 * SSM_H]
    wbig = jnp.concatenate([w_qkv, w_xbc], axis=1).astype(BF16)
    wz = jnp.concatenate([w_dz, w_sz], axis=1).astype(BF16)
    pad = jnp.zeros((D_MODEL, LANES - G_ACS - 2 * SSM_H), F32)
    wsm = jnp.concatenate([w_a, w_b, w_dt, w_dt, pad], axis=1)
    zrow = lambda n: jnp.zeros((n,), F32)
    brow = jnp.concatenate([dn_dt_bias.reshape(-1), zrow(2 * DN_H), ssm_dt_bias.reshape(-1),
                            ssm_dt_bias.reshape(-1), zrow(LANES - G_ACS - 2 * SSM_H)]).reshape(1, LANES)
    arow = jnp.concatenate([-jnp.exp(dn_A_log.reshape(-1)), zrow(2 * DN_H + 2 * SSM_H),
                            -jnp.exp(ssm_A_log.reshape(-1)), zrow(LANES - G_ACS - 2 * SSM_H)]).reshape(1, LANES)
    cw = jnp.concatenate([dn_conv_w, ssm_conv_w], axis=1)
    cb = jnp.concatenate([zrow(3 * DN_W), ssm_conv_b]).reshape(1, CONV_W)
    return dict(
        wbig=wbig, wz=wz, wsm=wsm, brow=brow, arow=arow, cw=cw, cb=cb,
        dn_nw=dn_norm_w.reshape(1, DN_DK), d_row=jnp.repeat(ssm_D, SSM_P).reshape(1, SSM_W),
        ssm_nw=ssm_norm_w.reshape(1, SSM_W), wout=w_out.astype(BF16),
        l1g=ln1_g.reshape(1, D_MODEL), l1b=ln1_b.reshape(1, D_MODEL),
        wg=w_gate.astype(BF16), wu=w_up.astype(BF16), wd=w_down.astype(BF16),
        l2g=ln2_g.reshape(1, D_MODEL), l2b=ln2_b.reshape(1, D_MODEL))


def _encoder_layer(x, mod, p, alpha):
    qkv, xbc, zz, gates, gates_t = _inproj_call(x, mod, p["wbig"], p["wz"], p["wsm"], p["cw"], p["cb"],
                                                p["brow"], p["arow"], tb=256)
    o_f = _dn_call(qkv, gates, gates_t, rev=False)
    o_dn = _dn_call(qkv, gates, gates_t, rev=True, o_fwd=o_f, zz=zz, norm_w=p["dn_nw"])
    y_f = _ssd_call(xbc, gates, gates_t, rev=False)
    o_ssm = _ssd_call(xbc, gates, gates_t, rev=True, y_fwd=y_f, zz=zz, d_row=p["d_row"], norm_w=p["ssm_nw"])
    return _tail_call(o_dn, o_ssm, x, mod, p["wout"], p["l1g"], p["l1b"], p["wg"], p["wu"], p["wd"],
                      p["l2g"], p["l2b"], tb=256, alpha=alpha)


def kernel(x_prompt, x_sample, c_prompt, c_sample, w_ada, b_ada, w_in, dn_conv_w, dn_A_log, dn_dt_bias, dn_norm_w, ssm_conv_w, ssm_conv_b, ssm_A_log, ssm_dt_bias, ssm_D, ssm_norm_w, w_out, ln1_g, ln1_b, w_gate, w_up, w_down, ln2_g, ln2_b):
    depth = w_ada.shape[0]
    alpha = (2 * depth) ** 0.25
    layer_params = (w_in, dn_conv_w, dn_A_log, dn_dt_bias, dn_norm_w, ssm_conv_w, ssm_conv_b, ssm_A_log,
                    ssm_dt_bias, ssm_D, ssm_norm_w, w_out, ln1_g, ln1_b, w_gate, w_up, w_down, ln2_g, ln2_b)
    nb_p, nb_s = c_prompt.shape[0], c_sample.shape[0]
    c_all = jnp.concatenate([c_prompt, c_sample], axis=0)
    nb = nb_p + nb_s
    c_pad = jnp.pad(c_all, ((0, (-nb) % 8), (0, 0)))
    xs = [x_prompt, x_sample]
    for l in range(depth):
        p = _prep_layer(*[w[l] for w in layer_params])
        mod = _mod_call(c_pad, w_ada[l], b_ada[l].reshape(1, -1))[:nb].reshape(nb, 6, D_MODEL)
        xs = [_encoder_layer(xs[0], mod[:nb_p], p, alpha), _encoder_layer(xs[1], mod[nb_p:], p, alpha)]
    return (xs[0], xs[1])
```

```python
import functools
import math

import jax
import jax.numpy as jnp
from jax import lax
from jax.experimental import pallas as pl
from jax.experimental.pallas import tpu as pltpu

F32 = jnp.float32
BF16 = jnp.bfloat16
HIGHEST = lax.Precision.HIGHEST

D_MODEL = 1024
DN_H = 4
DN_DK = 128
DN_W = DN_H * DN_DK
SSM_H = 8
SSM_P = 64
SSM_W = SSM_H * SSM_P
SSM_G = 2
SSM_N = 128
XBC_W = SSM_W + 2 * SSM_G * SSM_N
CONV_K = 5
D_FF = 2816
CONV_W = 3 * DN_W + XBC_W
LN_EPS = 1e-5
RMS_EPS = 1e-6

CH = 128
HALO = 16
LANES = 128

G_GC = 0
G_BETA = 8
G_DT = 16
G_ACS = 32

VMEM_LIMIT = 56 * 1024 * 1024


def _sigmoid(v):
    return 1.0 / (1.0 + jnp.exp(-v))


def _silu(v):
    return v * _sigmoid(v)


def _softplus(v):
    return jnp.maximum(v, 0.0) + jnp.log1p(jnp.exp(-jnp.abs(v)))


def _dot(a, b):
    return jnp.dot(a, b, preferred_element_type=F32)


def _dot_nt(a, b):
    return lax.dot_general(a, b, (((1,), (1,)), ((), ())), preferred_element_type=F32)


def _dot_tn(a, b):
    return lax.dot_general(a, b, (((0,), (0,)), ((), ())), preferred_element_type=F32)


def _const_spec(shape):
    nd = len(shape)
    return pl.BlockSpec(shape, lambda *_: (0,) * nd, pipeline_mode=pl.Buffered(1))


def _mod_kernel(c_ref, w_ref, b_ref, o_ref):
    s = _silu(c_ref[...])
    o_ref[...] = jnp.dot(s, w_ref[...], precision=HIGHEST, preferred_element_type=F32) + b_ref[...]


def _mod_call(c_pad, w_ada, b_ada):
    bp = c_pad.shape[0]
    n = w_ada.shape[1]
    blk = 1024
    return pl.pallas_call(
        _mod_kernel,
        grid=(n // blk,),
        in_specs=[pl.BlockSpec((bp, D_MODEL), lambda j: (0, 0)),
                  pl.BlockSpec((D_MODEL, blk), lambda j: (0, j)),
                  pl.BlockSpec((1, blk), lambda j: (0, j))],
        out_specs=pl.BlockSpec((bp, blk), lambda j: (0, j)),
        out_shape=jax.ShapeDtypeStruct((bp, n), F32),
        compiler_params=pltpu.CompilerParams(dimension_semantics=("arbitrary",), vmem_limit_bytes=VMEM_LIMIT),
        name="mod",
    )(c_pad, w_ada, b_ada)


def _split3(v):
    hi = v.astype(BF16)
    r1 = v - hi.astype(F32)
    mid = r1.astype(BF16)
    lo = (r1 - mid.astype(F32)).astype(BF16)
    return hi, mid, lo


def _inproj_kernel(xm_ref, xp_ref, xn_ref, mod_ref, wbig_ref, wzs_ref, cw_ref, cb_ref, brow_ref, arow_ref,
                   qkv_ref, xbc_ref, zz_ref, g_ref, gt_ref, pbuf, *, tb, nt):
    i = pl.program_id(1)
    sh = mod_ref[0, 0:1, :]
    sc = mod_ref[0, 1:2, :]

    def modulate(v):
        return v * (1.0 + sc) + sh

    keep_prev = jnp.where(i > 0, 1.0, 0.0)
    keep_next = jnp.where(i < nt - 1, 1.0, 0.0)
    hm16 = modulate(xm_ref[0]).astype(BF16)
    h_all = jnp.concatenate([(modulate(xp_ref[0]) * keep_prev).astype(BF16), hm16,
                             (modulate(xn_ref[0]) * keep_next).astype(BF16)], axis=0)

    gw = 2 * LANES
    n_groups = CONV_W // gw
    z_groups = 2 * DN_W // gw
    rb = 64
    base = HALO - CONV_K // 2

    def project(gi):
        pbuf[:, gi * gw:(gi + 1) * gw] = _dot(h_all, wbig_ref[:, gi * gw:(gi + 1) * gw])

    def conv(cg):
        cs = slice(cg * LANES, (cg + 1) * LANES)
        w = [cw_ref[j:j + 1, cs] for j in range(CONV_K)]
        bias = cb_ref[0:1, cs]
        for r0 in range(0, tb, rb):
            acc = bias + pbuf[base + r0:base + r0 + rb, cs] * w[0]
            for j in range(1, CONV_K):
                acc = acc + pbuf[base + j + r0:base + j + r0 + rb, cs] * w[j]
            y = _silu(acc)
            if cg < 2 * DN_H:
                y = y * lax.rsqrt(jnp.sum(y * y, axis=-1, keepdims=True) + RMS_EPS)
                if cg < DN_H:
                    y = y * (DN_DK ** -0.5)
            if cg < 3 * DN_H:
                qkv_ref[0, r0:r0 + rb, cs] = y
            else:
                c2 = cg - 3 * DN_H
                xbc_ref[0, r0:r0 + rb, c2 * LANES:(c2 + 1) * LANES] = y

    project(0)
    for gi in range(n_groups):
        if gi + 1 < n_groups:
            project(gi + 1)
        if gi < z_groups:
            zz_ref[0, :, gi * gw:(gi + 1) * gw] = _dot(hm16, wzs_ref[:, gi * gw:(gi + 1) * gw])
        for cg in range(gi * gw // LANES, (gi + 1) * gw // LANES):
            conv(cg)

    ps = _dot(hm16, wzs_ref[:, 2 * DN_W:2 * DN_W + LANES])
    lane = lax.broadcasted_iota(jnp.int32, (1, LANES), 1)
    sp = _softplus(ps + brow_ref[...])
    decay_src = sp * arow_ref[...]
    is_beta = (lane >= G_BETA) & (lane < G_DT)
    is_dt = (lane >= G_DT) & (lane < G_ACS)
    raw = jnp.where(is_beta, _sigmoid(ps), jnp.where(is_dt, sp, 0.0))
    fwd_cum = (lane < G_GC + DN_H) | ((lane >= G_ACS) & (lane < G_ACS + SSM_H))
    bwd_cum = ((lane >= G_GC + DN_H) & (lane < G_BETA)) | ((lane >= G_ACS + SSM_H) & (lane < G_ACS + 2 * SSM_H))
    r = lax.broadcasted_iota(jnp.int32, (CH, 2 * CH), 0)
    c = lax.broadcasted_iota(jnp.int32, (CH, 2 * CH), 1)
    tri2 = jnp.where(((c < CH) & (r >= c)) | ((c >= CH) & (r <= c - CH)), 1.0, 0.0).astype(BF16)
    for ck in range(tb // CH):
        rs = slice(ck * CH, (ck + 1) * CH)
        src = decay_src[rs]
        stacked = jnp.concatenate([jnp.where(fwd_cum, src, 0.0), jnp.where(bwd_cum, src, 0.0)], axis=0)
        hi, mid, lo = _split3(stacked)
        cum = _dot(tri2, hi) + _dot(tri2, mid) + _dot(tri2, lo)
        gates = jnp.where(fwd_cum | bwd_cum, cum, raw[rs])
        g_ref[0, rs, :] = gates
        gt_ref[0, :, rs] = gates.T


def _inproj_call(x, mod, wbig, wzs, cw, cb, brow, arow, tb):
    bsz, t, _ = x.shape
    nt = t // tb
    hb = tb // HALO
    nhb = t // HALO
    kern = functools.partial(_inproj_kernel, tb=tb, nt=nt)
    return pl.pallas_call(
        kern,
        grid=(bsz, nt),
        in_specs=[
            pl.BlockSpec((1, tb, D_MODEL), lambda b, i: (b, i, 0)),
            pl.BlockSpec((1, HALO, D_MODEL), lambda b, i: (b, jnp.maximum(i * hb - 1, 0), 0)),
            pl.BlockSpec((1, HALO, D_MODEL), lambda b, i: (b, jnp.minimum((i + 1) * hb, nhb - 1), 0)),
            pl.BlockSpec((1, 6, D_MODEL), lambda b, i: (b, 0, 0)),
            _const_spec((D_MODEL, CONV_W)),
            _const_spec((D_MODEL, 2 * DN_W + LANES)),
            _const_spec((CONV_K, CONV_W)),
            _const_spec((1, CONV_W)),
            _const_spec((1, LANES)),
            _const_spec((1, LANES)),
        ],
        out_specs=[
            pl.BlockSpec((1, tb, 3 * DN_W), lambda b, i: (b, i, 0)),
            pl.BlockSpec((1, tb, XBC_W), lambda b, i: (b, i, 0)),
            pl.BlockSpec((1, tb, 2 * DN_W), lambda b, i: (b, i, 0)),
            pl.BlockSpec((1, tb, LANES), lambda b, i: (b, i, 0)),
            pl.BlockSpec((1, LANES, tb), lambda b, i: (b, 0, i)),
        ],
        out_shape=[
            jax.ShapeDtypeStruct((bsz, t, 3 * DN_W), F32),
            jax.ShapeDtypeStruct((bsz, t, XBC_W), F32),
            jax.ShapeDtypeStruct((bsz, t, 2 * DN_W), F32),
            jax.ShapeDtypeStruct((bsz, t, LANES), F32),
            jax.ShapeDtypeStruct((bsz, LANES, t), F32),
        ],
        scratch_shapes=[pltpu.VMEM((tb + 2 * HALO, CONV_W), F32)],
        compiler_params=pltpu.CompilerParams(dimension_semantics=("arbitrary", "arbitrary"),
                                             vmem_limit_bytes=VMEM_LIMIT),
        name="inproj",
    )(x, x, x, mod, wbig, wzs, cw, cb, brow, arow)


def _dn_masks():
    r = jnp.arange(CH)[:, None]
    c = jnp.arange(CH)[None, :]
    ms = [(r >> 1) == (c >> 1)]
    for s in range(1, 7):
        ms.append(((r >> (s + 1)) == (c >> (s + 1))) & ((r >> s) != (c >> s)))
    ms.append(r == c)
    return jnp.stack(ms).astype(BF16)


def _dn_kernel(*refs, rev, final, nb):
    if final:
        qkv_ref, g_ref, gt_ref, m_ref, of_ref, z_ref, nw_ref, o_ref, s_ref, l16, x16, t16, a16 = refs
    else:
        qkv_ref, g_ref, gt_ref, m_ref, o_ref, s_ref, l16, x16, t16, a16 = refs

    @pl.when(pl.program_id(1) == 0)
    def _():
        s_ref[...] = jnp.zeros_like(s_ref)

    r = lax.broadcasted_iota(jnp.int32, (CH, CH), 0)
    c = lax.broadcasted_iota(jnp.int32, (CH, CH), 1)
    incl = (r <= c) if rev else (r >= c)
    strict = (r < c) if rev else (r > c)
    lo = DN_H if rev else 0
    last = 0 if rev else CH - 1
    chains = [(bi, h) for bi in range(nb) for h in range(DN_H)]
    n = len(chains)

    def q_of(bi, h):
        return qkv_ref[bi, :, h * DN_DK:(h + 1) * DN_DK]

    def k_of(bi, h):
        return qkv_ref[bi, :, DN_W + h * DN_DK:DN_W + (h + 1) * DN_DK]

    def v_of(bi, h):
        return qkv_ref[bi, :, 2 * DN_W + h * DN_DK:2 * DN_W + (h + 1) * DN_DK]

    def gcc_of(bi, h):
        return g_ref[bi, :, G_GC + lo + h:G_GC + lo + h + 1]

    def beta_of(bi, h):
        return g_ref[bi, :, G_BETA + lo + h:G_BETA + lo + h + 1]

    for ci, (bi, h) in enumerate(chains):
        gi = G_GC + lo + h
        k = k_of(bi, h)
        decay = jnp.exp(jnp.where(incl, gcc_of(bi, h) - gt_ref[bi, gi:gi + 1, :], -jnp.inf))
        k16 = k.astype(BF16)
        lm = jnp.where(strict, _dot_nt((k * beta_of(bi, h)).astype(BF16), k16) * decay, 0.0).astype(BF16)
        l16[ci] = lm
        x16[ci] = m_ref[7] - lm * m_ref[0]
        a16[ci] = (_dot_nt(q_of(bi, h).astype(BF16), k16) * decay).astype(BF16)
    for s in range(1, 7):
        for ci in range(n):
            t16[ci] = _dot(l16[ci] * m_ref[s], x16[ci]).astype(BF16)
        for ci in range(n):
            x16[ci] = x16[ci] - _dot(x16[ci], t16[ci]).astype(BF16)
    for ci, (bi, h) in enumerate(chains):
        kg16 = (k_of(bi, h) * jnp.exp(gcc_of(bi, h))).astype(BF16)
        resid = v_of(bi, h) - _dot(kg16, s_ref[ci].astype(BF16))
        t16[ci] = (beta_of(bi, h) * resid).astype(BF16)
    for ci in range(n):
        l16[ci] = _dot(x16[ci], t16[ci]).astype(BF16)
    for ci, (bi, h) in enumerate(chains):
        hs = slice(h * DN_DK, (h + 1) * DN_DK)
        gi = G_GC + lo + h
        gcc = gcc_of(bi, h)
        glast = g_ref[bi, last:last + 1, gi:gi + 1]
        st = s_ref[ci]
        vnew16 = l16[ci]
        qg16 = (q_of(bi, h) * jnp.exp(gcc)).astype(BF16)
        o = _dot(jnp.concatenate([qg16, a16[ci]], axis=1), jnp.concatenate([st.astype(BF16), vnew16], axis=0))
        kdec16 = (k_of(bi, h) * jnp.exp(glast - gcc)).astype(BF16)
        s_ref[ci] = st * jnp.exp(glast) + _dot_tn(kdec16, vnew16)
        if final:
            o = o + of_ref[bi, :, hs]
            o = o * lax.rsqrt(jnp.mean(o * o, axis=-1, keepdims=True) + RMS_EPS)
            o = o * nw_ref[...] * _silu(z_ref[bi, :, hs])
        o_ref[bi, :, hs] = o


def _dn_call(qkv, gates, gates_t, masks, rev, nb, o_fwd=None, zz=None, norm_w=None):
    bsz, t, _ = qkv.shape
    nc = t // CH
    final = o_fwd is not None

    def cidx(i):
        return (nc - 1 - i) if rev else i

    in_specs = [
        pl.BlockSpec((nb, CH, 3 * DN_W), lambda b, i: (b, cidx(i), 0)),
        pl.BlockSpec((nb, CH, LANES), lambda b, i: (b, cidx(i), 0)),
        pl.BlockSpec((nb, LANES, CH), lambda b, i: (b, 0, cidx(i))),
        _const_spec((8, CH, CH)),
    ]
    args = [qkv, gates, gates_t, masks]
    if final:
        in_specs += [
            pl.BlockSpec((nb, CH, DN_W), lambda b, i: (b, cidx(i), 0)),
            pl.BlockSpec((nb, CH, DN_W), lambda b, i: (b, cidx(i), 0)),
            _const_spec((1, DN_DK)),
        ]
        args += [o_fwd, zz, norm_w]
    n = nb * DN_H
    return pl.pallas_call(
        functools.partial(_dn_kernel, rev=rev, final=final, nb=nb),
        grid=(bsz // nb, nc),
        in_specs=in_specs,
        out_specs=pl.BlockSpec((nb, CH, DN_W), lambda b, i: (b, cidx(i), 0)),
        out_shape=jax.ShapeDtypeStruct((bsz, t, DN_W), F32),
        scratch_shapes=[pltpu.VMEM((n, DN_DK, DN_DK), F32)] + [pltpu.VMEM((n, CH, CH), BF16)] * 4,
        compiler_params=pltpu.CompilerParams(dimension_semantics=("arbitrary", "arbitrary"),
                                             vmem_limit_bytes=VMEM_LIMIT),
        name="dn_bwd" if rev else "dn_fwd",
    )(*args)


def _ssd_kernel(*refs, rev, final, nb):
    if final:
        xbc_ref, g_ref, gt_ref, yf_ref, z_ref, d_ref, nw_ref, y_ref, s_ref = refs
    else:
        xbc_ref, g_ref, gt_ref, y_ref, s_ref = refs

    @pl.when(pl.program_id(1) == 0)
    def _():
        s_ref[...] = jnp.zeros_like(s_ref)

    r = lax.broadcasted_iota(jnp.int32, (CH, CH), 0)
    c = lax.broadcasted_iota(jnp.int32, (CH, CH), 1)
    incl = (r <= c) if rev else (r >= c)
    left = lax.broadcasted_iota(jnp.int32, (1, LANES), 1) < SSM_P
    lo = SSM_H if rev else 0
    last = 0 if rev else CH - 1
    pairs_per_group = SSM_H // SSM_G // 2
    n_pairs = SSM_H // 2

    def b16_of(bi, g):
        return xbc_ref[bi, :, SSM_W + g * SSM_N:SSM_W + (g + 1) * SSM_N].astype(BF16)

    def c16_of(bi, g):
        o = SSM_W + SSM_G * SSM_N
        return xbc_ref[bi, :, o + g * SSM_N:o + (g + 1) * SSM_N].astype(BF16)

    cbs, yoffs = {}, {}
    for bi in range(nb):
        for g in range(SSM_G):
            c16 = c16_of(bi, g)
            cbs[bi, g] = _dot_nt(c16, b16_of(bi, g))
            for pp in range(pairs_per_group):
                p = g * pairs_per_group + pp
                yoffs[bi, p] = _dot(c16, s_ref[bi * n_pairs + p].astype(BF16))
    for bi in range(nb):
        for g in range(SSM_G):
            b16 = b16_of(bi, g)
            cb = cbs[bi, g]
            ys = []
            for pp in range(pairs_per_group):
                p = g * pairs_per_group + pp
                ps = slice(p * LANES, (p + 1) * LANES)
                x = xbc_ref[bi, :, ps]
                d0 = G_DT + lo + 2 * p
                a0 = G_ACS + lo + 2 * p
                dt = jnp.where(left, g_ref[bi, :, d0:d0 + 1], g_ref[bi, :, d0 + 1:d0 + 2])
                ac0, ac1 = g_ref[bi, :, a0:a0 + 1], g_ref[bi, :, a0 + 1:a0 + 2]
                ar0, ar1 = gt_ref[bi, a0:a0 + 1, :], gt_ref[bi, a0 + 1:a0 + 2, :]
                l0 = jnp.exp(jnp.where(incl, ac0 - ar0, -jnp.inf))
                l1 = jnp.exp(jnp.where(incl, ac1 - ar1, -jnp.inf))
                xdt = x * dt
                m01 = jnp.concatenate([(cb * l0).astype(BF16), (cb * l1).astype(BF16)], axis=1)
                x01 = jnp.concatenate([jnp.where(left, xdt, 0.0).astype(BF16),
                                       jnp.where(left, 0.0, xdt).astype(BF16)], axis=0)
                acp = jnp.where(left, ac0, ac1)
                alp = jnp.where(left, g_ref[bi, last:last + 1, a0:a0 + 1],
                                g_ref[bi, last:last + 1, a0 + 1:a0 + 2])
                y = _dot(m01, x01) + yoffs[bi, p] * jnp.exp(acp)
                si = bi * n_pairs + p
                s_ref[si] = s_ref[si] * jnp.exp(alp) + _dot_tn(b16, (xdt * jnp.exp(alp - acp)).astype(BF16))
                if final:
                    y = y + yf_ref[bi, :, ps] + d_ref[0:1, ps] * x
                    y = y * _silu(z_ref[bi, :, ps])
                ys.append((ps, y))
            if final:
                ssq = sum(jnp.sum(y * y, axis=-1, keepdims=True) for _, y in ys)
                inv = lax.rsqrt(ssq / (SSM_W // SSM_G) + RMS_EPS)
                for ps, y in ys:
                    y_ref[bi, :, ps] = y * inv * nw_ref[0:1, ps]
            else:
                for ps, y in ys:
                    y_ref[bi, :, ps] = y


def _ssd_call(xbc, gates, gates_t, rev, nb, y_fwd=None, zz=None, d_row=None, norm_w=None):
    bsz, t, _ = xbc.shape
    nc = t // CH
    final = y_fwd is not None

    def cidx(i):
        return (nc - 1 - i) if rev else i

    in_specs = [
        pl.BlockSpec((nb, CH, XBC_W), lambda b, i: (b, cidx(i), 0)),
        pl.BlockSpec((nb, CH, LANES), lambda b, i: (b, cidx(i), 0)),
        pl.BlockSpec((nb, LANES, CH), lambda b, i: (b, 0, cidx(i))),
    ]
    args = [xbc, gates, gates_t]
    if final:
        in_specs += [
            pl.BlockSpec((nb, CH, SSM_W), lambda b, i: (b, cidx(i), 0)),
            pl.BlockSpec((nb, CH, SSM_W), lambda b, i: (b, cidx(i), 1)),
            _const_spec((1, SSM_W)),
            _const_spec((1, SSM_W)),
        ]
        args += [y_fwd, zz, d_row, norm_w]
    return pl.pallas_call(
        functools.partial(_ssd_kernel, rev=rev, final=final, nb=nb),
        grid=(bsz // nb, nc),
        in_specs=in_specs,
        out_specs=pl.BlockSpec((nb, CH, SSM_W), lambda b, i: (b, cidx(i), 0)),
        out_shape=jax.ShapeDtypeStruct((bsz, t, SSM_W), F32),
        scratch_shapes=[pltpu.VMEM((nb * SSM_H // 2, SSM_N, 2 * SSM_P), F32)],
        compiler_params=pltpu.CompilerParams(dimension_semantics=("arbitrary", "arbitrary"),
                                             vmem_limit_bytes=VMEM_LIMIT),
        name="ssd_bwd" if rev else "ssd_fwd",
    )(*args)


def _layer_norm(v, g, b):
    mu = jnp.mean(v, axis=-1, keepdims=True)
    d = v - mu
    var = jnp.mean(d * d, axis=-1, keepdims=True)
    return d * lax.rsqrt(var + LN_EPS) * g + b


def _tail_kernel(odn_ref, ossm_ref, x_ref, mod_ref, wout_ref, l1g_ref, l1b_ref, wg_ref, wu_ref, wd_ref,
                 l2g_ref, l2b_ref, out_ref, *, alpha):
    gt1 = mod_ref[0, 2:3, :]
    sh2 = mod_ref[0, 3:4, :]
    sc2 = mod_ref[0, 4:5, :]
    gt2 = mod_ref[0, 5:6, :]
    mix = (_dot(odn_ref[0].astype(BF16), wout_ref[0:DN_W, :])
           + _dot(ossm_ref[0].astype(BF16), wout_ref[DN_W:DN_W + SSM_W, :]))
    x1 = _layer_norm(alpha * x_ref[0] + gt1 * mix, l1g_ref[...], l1b_ref[...])
    h16 = (x1 * (1.0 + sc2) + sh2).astype(BF16)
    act = (_silu(_dot(h16, wg_ref[...])) * _dot(h16, wu_ref[...])).astype(BF16)
    ffn = _dot(act, wd_ref[...])
    out_ref[0] = _layer_norm(alpha * x1 + gt2 * ffn, l2g_ref[...], l2b_ref[...])


def _tail_call(o_dn, o_ssm, x, mod, wout, l1g, l1b, wg, wu, wd, l2g, l2b, tb, alpha):
    bsz, t, _ = x.shape
    nt = t // tb
    return pl.pallas_call(
        functools.partial(_tail_kernel, alpha=alpha),
        grid=(bsz, nt),
        in_specs=[
            pl.BlockSpec((1, tb, DN_W), lambda b, i: (b, i, 0)),
            pl.BlockSpec((1, tb, SSM_W), lambda b, i: (b, i, 0)),
            pl.BlockSpec((1, tb, D_MODEL), lambda b, i: (b, i, 0)),
            pl.BlockSpec((1, 6, D_MODEL), lambda b, i: (b, 0, 0)),
            _const_spec((DN_W + SSM_W, D_MODEL)),
            _const_spec((1, D_MODEL)),
            _const_spec((1, D_MODEL)),
            _const_spec((D_MODEL, D_FF)),
            _const_spec((D_MODEL, D_FF)),
            _const_spec((D_FF, D_MODEL)),
            _const_spec((1, D_MODEL)),
            _const_spec((1, D_MODEL)),
        ],
        out_specs=pl.BlockSpec((1, tb, D_MODEL), lambda b, i: (b, i, 0)),
        out_shape=jax.ShapeDtypeStruct((bsz, t, D_MODEL), F32),
        compiler_params=pltpu.CompilerParams(dimension_semantics=("arbitrary", "arbitrary"),
                                             vmem_limit_bytes=VMEM_LIMIT),
        name="tail",
    )(o_dn, o_ssm, x, mod, wout, l1g, l1b, wg, wu, wd, l2g, l2b)


def _prep_layer(w_in, dn_conv_w, dn_A_log, dn_dt_bias, dn_norm_w, ssm_conv_w, ssm_conv_b, ssm_A_log,
                ssm_dt_bias, ssm_D, ssm_norm_w, w_out, ln1_g, ln1_b, w_gate, w_up, w_down, ln2_g, ln2_b):
    o = 0
    w_qkv = w_in[:, o:o + 3 * DN_W]; o += 3 * DN_W
    w_dz = w_in[:, o:o + DN_W]; o += DN_W
    w_a = w_in[:, o:o + 2 * DN_H]; o += 2 * DN_H
    w_b = w_in[:, o:o + 2 * DN_H]; o += 2 * DN_H
    w_sz = w_in[:, o:o + SSM_W]; o += SSM_W
    w_xbc = w_in[:, o:o + XBC_W]; o += XBC_W
    w_dt = w_in[:, o:o + 2 * SSM_H]
    wbig = jnp.concatenate([w_qkv, w_xbc], axis=1).astype(BF16)
    pad = jnp.zeros((D_MODEL, LANES - G_ACS - 2 * SSM_H), F32)
    wzs = jnp.concatenate([w_dz, w_sz, w_a, w_b, w_dt, w_dt, pad], axis=1).astype(BF16)
    zrow = lambda n: jnp.zeros((n,), F32)
    brow = jnp.concatenate([dn_dt_bias.reshape(-1), zrow(2 * DN_H), ssm_dt_bias.reshape(-1),
                            ssm_dt_bias.reshape(-1), zrow(LANES - G_ACS - 2 * SSM_H)]).reshape(1, LANES)
    arow = jnp.concatenate([-jnp.exp(dn_A_log.reshape(-1)), zrow(2 * DN_H + 2 * SSM_H),
                            -jnp.exp(ssm_A_log.reshape(-1)), zrow(LANES - G_ACS - 2 * SSM_H)]).reshape(1, LANES)
    cw = jnp.concatenate([dn_conv_w, ssm_conv_w], axis=1)
    cb = jnp.concatenate([zrow(3 * DN_W), ssm_conv_b]).reshape(1, CONV_W)
    return dict(
        wbig=wbig, wzs=wzs, brow=brow, arow=arow, cw=cw, cb=cb,
        dn_nw=dn_norm_w.reshape(1, DN_DK), d_row=jnp.repeat(ssm_D, SSM_P).reshape(1, SSM_W),
        ssm_nw=ssm_norm_w.reshape(1, SSM_W), wout=w_out.astype(BF16),
        l1g=ln1_g.reshape(1, D_MODEL), l1b=ln1_b.reshape(1, D_MODEL),
        wg=w_gate.astype(BF16), wu=w_up.astype(BF16), wd=w_down.astype(BF16),
        l2g=ln2_g.reshape(1, D_MODEL), l2b=ln2_b.reshape(1, D_MODEL))


def _encoder_layer(x, mod, p, alpha):
    qkv, xbc, zz, gates, gates_t = _inproj_call(x, mod, p["wbig"], p["wzs"], p["cw"], p["cb"],
                                                p["brow"], p["arow"], tb=256)
    nb = math.gcd(x.shape[0], 4)
    masks = _dn_masks()
    o_f = _dn_call(qkv, gates, gates_t, masks, rev=False, nb=nb)
    o_dn = _dn_call(qkv, gates, gates_t, masks, rev=True, nb=nb, o_fwd=o_f, zz=zz, norm_w=p["dn_nw"])
    y_f = _ssd_call(xbc, gates, gates_t, rev=False, nb=nb)
    o_ssm = _ssd_call(xbc, gates, gates_t, rev=True, nb=nb, y_fwd=y_f, zz=zz, d_row=p["d_row"],
                      norm_w=p["ssm_nw"])
    return _tail_call(o_dn, o_ssm, x, mod, p["wout"], p["l1g"], p["l1b"], p["wg"], p["wu"], p["wd"],
                      p["l2g"], p["l2b"], tb=256, alpha=alpha)


def kernel(x_prompt, x_sample, c_prompt, c_sample, w_ada, b_ada, w_in, dn_conv_w, dn_A_log, dn_dt_bias, dn_norm_w, ssm_conv_w, ssm_conv_b, ssm_A_log, ssm_dt_bias, ssm_D, ssm_norm_w, w_out, ln1_g, ln1_b, w_gate, w_up, w_down, ln2_g, ln2_b):
    depth = w_ada.shape[0]
    alpha = (2 * depth) ** 0.25
    layer_params = (w_in, dn_conv_w, dn_A_log, dn_dt_bias, dn_norm_w, ssm_conv_w, ssm_conv_b, ssm_A_log,
                    ssm_dt_bias, ssm_D, ssm_norm_w, w_out, ln1_g, ln1_b, w_gate, w_up, w_down, ln2_g, ln2_b)
    nb_p, nb_s = c_prompt.shape[0], c_sample.shape[0]
    c_all = jnp.concatenate([c_prompt, c_sample], axis=0)
    nb = nb_p + nb_s
    c_pad = jnp.pad(c_all, ((0, (-nb) % 8), (0, 0)))
    xs = [x_prompt, x_sample]
    for l in range(depth):
        p = _prep_layer(*[w[l] for w in layer_params])
        mod = _mod_call(c_pad, w_ada[l], b_ada[l].reshape(1, -1))[:nb].reshape(nb, 6, D_MODEL)
        xs = [_encoder_layer(xs[0], mod[:nb_p], p, alpha), _encoder_layer(xs[1], mod[nb_p:], p, alpha)]
    return (xs[0], xs[1])
```

```python
import functools
import math

import jax
import jax.numpy as jnp
from jax import lax
from jax.experimental import pallas as pl
from jax.experimental.pallas import tpu as pltpu

F32 = jnp.float32
BF16 = jnp.bfloat16
HIGHEST = lax.Precision.HIGHEST

D_MODEL = 1024
DN_H = 4
DN_DK = 128
DN_W = DN_H * DN_DK
SSM_H = 8
SSM_P = 64
SSM_W = SSM_H * SSM_P
SSM_G = 2
SSM_N = 128
XBC_W = SSM_W + 2 * SSM_G * SSM_N
CONV_K = 5
D_FF = 2816
CONV_W = 3 * DN_W + XBC_W
LN_EPS = 1e-5
RMS_EPS = 1e-6

CH = 128
HALO = 8
LANES = 128

G_GC = 0
G_BETA = 8
G_DT = 16
G_ACS = 32

VMEM_LIMIT = 56 * 1024 * 1024


def _sigmoid(v):
    return 1.0 / (1.0 + jnp.exp(-v))


def _silu(v):
    return v * _sigmoid(v)


def _softplus(v):
    return jnp.maximum(v, 0.0) + jnp.log1p(jnp.exp(-jnp.abs(v)))


def _dot(a, b):
    return jnp.dot(a, b, preferred_element_type=F32)


def _dot_nt(a, b):
    return lax.dot_general(a, b, (((1,), (1,)), ((), ())), preferred_element_type=F32)


def _dot_tn(a, b):
    return lax.dot_general(a, b, (((0,), (0,)), ((), ())), preferred_element_type=F32)


def _const_spec(shape):
    nd = len(shape)
    return pl.BlockSpec(shape, lambda *_: (0,) * nd, pipeline_mode=pl.Buffered(1))


def _mod_kernel(c_ref, w_ref, b_ref, o_ref):
    s = _silu(c_ref[...])
    o_ref[...] = jnp.dot(s, w_ref[...], precision=HIGHEST, preferred_element_type=F32) + b_ref[...]


def _mod_call(c_pad, w_ada, b_ada):
    bp = c_pad.shape[0]
    n = w_ada.shape[1]
    blk = 1024
    return pl.pallas_call(
        _mod_kernel,
        grid=(n // blk,),
        in_specs=[pl.BlockSpec((bp, D_MODEL), lambda j: (0, 0)),
                  pl.BlockSpec((D_MODEL, blk), lambda j: (0, j)),
                  pl.BlockSpec((1, blk), lambda j: (0, j))],
        out_specs=pl.BlockSpec((bp, blk), lambda j: (0, j)),
        out_shape=jax.ShapeDtypeStruct((bp, n), F32),
        compiler_params=pltpu.CompilerParams(dimension_semantics=("arbitrary",), vmem_limit_bytes=VMEM_LIMIT),
        name="mod",
    )(c_pad, w_ada, b_ada)


def _split3(v):
    hi = v.astype(BF16)
    r1 = v - hi.astype(F32)
    mid = r1.astype(BF16)
    lo = (r1 - mid.astype(F32)).astype(BF16)
    return hi, mid, lo


def _inproj_kernel(xm_ref, xp_ref, xn_ref, mod_ref, wbig_ref, wzs_ref, cw_ref, cb_ref, brow_ref, arow_ref,
                   qkv_ref, kt_ref, xbc_ref, zz_ref, g_ref, gt_ref, hbuf, pbuf, ybuf, *, tb, nt):
    i = pl.program_id(1)
    sh = mod_ref[0, 0:1, :]
    sc = mod_ref[0, 1:2, :]

    def modulate(v):
        return v * (1.0 + sc) + sh

    keep_prev = jnp.where(i > 0, 1.0, 0.0)
    keep_next = jnp.where(i < nt - 1, 1.0, 0.0)
    hp = modulate(xp_ref[0]) * keep_prev
    hm = modulate(xm_ref[0])
    hn = modulate(xn_ref[0]) * keep_next
    hm16 = hm.astype(BF16)

    nv = (tb + 2 * HALO) // 8
    for lg in range(D_MODEL // LANES):
        ls = slice(lg * LANES, (lg + 1) * LANES)
        hbuf[lg, 0:HALO, :] = hp[:, ls]
        hbuf[lg, HALO:HALO + tb, :] = hm[:, ls]
        hbuf[lg, HALO + tb:2 * HALO + tb, :] = hn[:, ls]
    h_perm = jnp.concatenate(
        [jnp.concatenate([hbuf[lg, pl.ds(v, 8, stride=nv), :] for lg in range(D_MODEL // LANES)], axis=1)
         for v in range(nv)], axis=0).astype(BF16)

    gw = 2 * LANES
    n_groups = CONV_W // gw
    z_groups = 2 * DN_W // gw

    def project(gi):
        pbuf[:, gi * gw:(gi + 1) * gw] = _dot(h_perm, wbig_ref[:, gi * gw:(gi + 1) * gw])

    def tile(cs, u):
        if u < 0:
            return pltpu.roll(pbuf[(u + nv) * 8:(u + nv + 1) * 8, cs], 1, axis=0)
        if u >= nv:
            return pltpu.roll(pbuf[(u - nv) * 8:(u - nv + 1) * 8, cs], 7, axis=0)
        return pbuf[u * 8:(u + 1) * 8, cs]

    def conv(cg):
        cs = slice(cg * LANES, (cg + 1) * LANES)
        w = [cw_ref[j:j + 1, cs] for j in range(CONV_K)]
        bias = cb_ref[0:1, cs]
        for v in range(nv):
            acc = bias + tile(cs, v - CONV_K // 2) * w[0]
            for j in range(1, CONV_K):
                acc = acc + tile(cs, v + j - CONV_K // 2) * w[j]
            y = _silu(acc)
            if cg < 2 * DN_H:
                y = y * lax.rsqrt(jnp.sum(y * y, axis=-1, keepdims=True) + RMS_EPS)
                if cg < DN_H:
                    y = y * (DN_DK ** -0.5)
            ybuf[cg, pl.ds(v, 8, stride=nv), :] = y

    def emit(cg):
        y = ybuf[cg, HALO:HALO + tb, :]
        if cg < 3 * DN_H:
            qkv_ref[0, :, cg * LANES:(cg + 1) * LANES] = y
        else:
            c2 = cg - 3 * DN_H
            xbc_ref[0, :, c2 * LANES:(c2 + 1) * LANES] = y
        if DN_H <= cg < 2 * DN_H:
            kt_ref[0, (cg - DN_H) * DN_DK:(cg - DN_H + 1) * DN_DK, :] = y.T

    project(0)
    for gi in range(n_groups):
        if gi + 1 < n_groups:
            project(gi + 1)
        if gi < z_groups:
            zz_ref[0, :, gi * gw:(gi + 1) * gw] = _dot(hm16, wzs_ref[:, gi * gw:(gi + 1) * gw])
        for cg in range(gi * gw // LANES, (gi + 1) * gw // LANES):
            conv(cg)
            emit(cg)

    ps = _dot(hm16, wzs_ref[:, 2 * DN_W:2 * DN_W + LANES])
    lane = lax.broadcasted_iota(jnp.int32, (1, LANES), 1)
    sp = _softplus(ps + brow_ref[...])
    decay_src = sp * arow_ref[...]
    is_beta = (lane >= G_BETA) & (lane < G_DT)
    is_dt = (lane >= G_DT) & (lane < G_ACS)
    raw = jnp.where(is_beta, _sigmoid(ps), jnp.where(is_dt, sp, 0.0))
    fwd_cum = (lane < G_GC + DN_H) | ((lane >= G_ACS) & (lane < G_ACS + SSM_H))
    bwd_cum = ((lane >= G_GC + DN_H) & (lane < G_BETA)) | ((lane >= G_ACS + SSM_H) & (lane < G_ACS + 2 * SSM_H))
    r = lax.broadcasted_iota(jnp.int32, (CH, 2 * CH), 0)
    c = lax.broadcasted_iota(jnp.int32, (CH, 2 * CH), 1)
    tri2 = jnp.where(((c < CH) & (r >= c)) | ((c >= CH) & (r <= c - CH)), 1.0, 0.0).astype(BF16)
    for ck in range(tb // CH):
        rs = slice(ck * CH, (ck + 1) * CH)
        src = decay_src[rs]
        stacked = jnp.concatenate([jnp.where(fwd_cum, src, 0.0), jnp.where(bwd_cum, src, 0.0)], axis=0)
        hi, mid, lo = _split3(stacked)
        cum = _dot(tri2, hi) + _dot(tri2, mid) + _dot(tri2, lo)
        gates = jnp.where(fwd_cum | bwd_cum, cum, raw[rs])
        g_ref[0, rs, :] = gates
        gt_ref[0, :, rs] = gates.T


def _inproj_call(x, mod, wbig, wzs, cw, cb, brow, arow, tb):
    bsz, t, _ = x.shape
    nt = t // tb
    hb = tb // HALO
    nhb = t // HALO
    kern = functools.partial(_inproj_kernel, tb=tb, nt=nt)
    return pl.pallas_call(
        kern,
        grid=(bsz, nt),
        in_specs=[
            pl.BlockSpec((1, tb, D_MODEL), lambda b, i: (b, i, 0)),
            pl.BlockSpec((1, HALO, D_MODEL), lambda b, i: (b, jnp.maximum(i * hb - 1, 0), 0)),
            pl.BlockSpec((1, HALO, D_MODEL), lambda b, i: (b, jnp.minimum((i + 1) * hb, nhb - 1), 0)),
            pl.BlockSpec((1, 6, D_MODEL), lambda b, i: (b, 0, 0)),
            _const_spec((D_MODEL, CONV_W)),
            _const_spec((D_MODEL, 2 * DN_W + LANES)),
            _const_spec((CONV_K, CONV_W)),
            _const_spec((1, CONV_W)),
            _const_spec((1, LANES)),
            _const_spec((1, LANES)),
        ],
        out_specs=[
            pl.BlockSpec((1, tb, 3 * DN_W), lambda b, i: (b, i, 0)),
            pl.BlockSpec((1, DN_W, tb), lambda b, i: (b, 0, i)),
            pl.BlockSpec((1, tb, XBC_W), lambda b, i: (b, i, 0)),
            pl.BlockSpec((1, tb, 2 * DN_W), lambda b, i: (b, i, 0)),
            pl.BlockSpec((1, tb, LANES), lambda b, i: (b, i, 0)),
            pl.BlockSpec((1, LANES, tb), lambda b, i: (b, 0, i)),
        ],
        out_shape=[
            jax.ShapeDtypeStruct((bsz, t, 3 * DN_W), F32),
            jax.ShapeDtypeStruct((bsz, DN_W, t), F32),
            jax.ShapeDtypeStruct((bsz, t, XBC_W), F32),
            jax.ShapeDtypeStruct((bsz, t, 2 * DN_W), F32),
            jax.ShapeDtypeStruct((bsz, t, LANES), F32),
            jax.ShapeDtypeStruct((bsz, LANES, t), F32),
        ],
        scratch_shapes=[pltpu.VMEM((D_MODEL // LANES, tb + 2 * HALO, LANES), F32),
                        pltpu.VMEM((tb + 2 * HALO, CONV_W), F32),
                        pltpu.VMEM((CONV_W // LANES, tb + 2 * HALO, LANES), F32)],
        compiler_params=pltpu.CompilerParams(dimension_semantics=("arbitrary", "arbitrary"),
                                             vmem_limit_bytes=VMEM_LIMIT),
        name="inproj",
    )(x, x, x, mod, wbig, wzs, cw, cb, brow, arow)


def _dn_masks():
    r = jnp.arange(CH)[:, None]
    c = jnp.arange(CH)[None, :]
    ms = [(r >> 1) == (c >> 1)]
    for s in range(1, 7):
        ms.append(((r >> (s + 1)) == (c >> (s + 1))) & ((r >> s) != (c >> s)))
    ms.append(r == c)
    return jnp.stack(ms).astype(BF16)


def _dn_kernel(*refs, rev, final, nb):
    if final:
        qkv_ref, kt_ref, g_ref, gt_ref, m_ref, of_ref, z_ref, nw_ref, o_ref, s_ref, gb_s, bb_s, l16, x16, t16, a16 = refs
    else:
        qkv_ref, kt_ref, g_ref, gt_ref, m_ref, o_ref, s_ref, gb_s, bb_s, l16, x16, t16, a16 = refs

    @pl.when(pl.program_id(1) == 0)
    def _():
        s_ref[...] = jnp.zeros_like(s_ref)

    r = lax.broadcasted_iota(jnp.int32, (CH, CH), 0)
    c = lax.broadcasted_iota(jnp.int32, (CH, CH), 1)
    incl = (r <= c) if rev else (r >= c)
    strict = (r < c) if rev else (r > c)
    lo = DN_H if rev else 0
    last = 0 if rev else CH - 1
    chains = [(bi, h) for bi in range(nb) for h in range(DN_H)]
    n = len(chains)

    def q_of(bi, h):
        return qkv_ref[bi, :, h * DN_DK:(h + 1) * DN_DK]

    def k_of(bi, h):
        return qkv_ref[bi, :, DN_W + h * DN_DK:DN_W + (h + 1) * DN_DK]

    def v_of(bi, h):
        return qkv_ref[bi, :, 2 * DN_W + h * DN_DK:2 * DN_W + (h + 1) * DN_DK]

    def kt_of(bi, h):
        return kt_ref[bi, h * DN_DK:(h + 1) * DN_DK, :]

    def gcc_of(bi, h):
        return g_ref[bi, :, G_GC + lo + h:G_GC + lo + h + 1]

    def beta_of(bi, h):
        return g_ref[bi, :, G_BETA + lo + h:G_BETA + lo + h + 1]

    for ci, (bi, h) in enumerate(chains):
        gi = G_GC + lo + h
        gb = jnp.broadcast_to(gcc_of(bi, h), (CH, CH))
        bb = jnp.broadcast_to(beta_of(bi, h), (CH, CH))
        gb_s[ci] = gb
        bb_s[ci] = bb
        decay = jnp.exp(jnp.where(incl, gb - gt_ref[bi, gi:gi + 1, :], -jnp.inf))
        kt16 = kt_of(bi, h).astype(BF16)
        lm = jnp.where(strict, _dot((k_of(bi, h) * bb).astype(BF16), kt16) * decay, 0.0).astype(BF16)
        l16[ci] = lm
        x16[ci] = m_ref[7] - lm * m_ref[0]
        a16[ci] = (_dot(q_of(bi, h).astype(BF16), kt16) * decay).astype(BF16)
    for s in range(1, 7):
        for ci in range(n):
            t16[ci] = _dot(l16[ci] * m_ref[s], x16[ci]).astype(BF16)
        for ci in range(n):
            x16[ci] = x16[ci] - _dot(x16[ci], t16[ci]).astype(BF16)
    for ci, (bi, h) in enumerate(chains):
        kg16 = (k_of(bi, h) * jnp.exp(gb_s[ci])).astype(BF16)
        resid = v_of(bi, h) - _dot(kg16, s_ref[ci].astype(BF16))
        t16[ci] = (bb_s[ci] * resid).astype(BF16)
    for ci in range(n):
        l16[ci] = _dot(x16[ci], t16[ci]).astype(BF16)
    for ci, (bi, h) in enumerate(chains):
        hs = slice(h * DN_DK, (h + 1) * DN_DK)
        gi = G_GC + lo + h
        glast = g_ref[bi, last:last + 1, gi:gi + 1]
        st = s_ref[ci]
        vnew16 = l16[ci]
        qg16 = (q_of(bi, h) * jnp.exp(gb_s[ci])).astype(BF16)
        o = _dot(jnp.concatenate([qg16, a16[ci]], axis=1), jnp.concatenate([st.astype(BF16), vnew16], axis=0))
        kdec_t16 = (kt_of(bi, h) * jnp.exp(glast - gt_ref[bi, gi:gi + 1, :])).astype(BF16)
        s_ref[ci] = st * jnp.exp(glast) + _dot(kdec_t16, vnew16)
        if final:
            o = o + of_ref[bi, :, hs]
            o = o * lax.rsqrt(jnp.mean(o * o, axis=-1, keepdims=True) + RMS_EPS)
            o = o * nw_ref[...] * _silu(z_ref[bi, :, hs])
        o_ref[bi, :, hs] = o.astype(o_ref.dtype)


def _dn_call(qkv, k_t, gates, gates_t, masks, rev, nb, o_fwd=None, zz=None, norm_w=None):
    bsz, t, _ = qkv.shape
    nc = t // CH
    final = o_fwd is not None

    def cidx(i):
        return (nc - 1 - i) if rev else i

    in_specs = [
        pl.BlockSpec((nb, CH, 3 * DN_W), lambda b, i: (b, cidx(i), 0)),
        pl.BlockSpec((nb, DN_W, CH), lambda b, i: (b, 0, cidx(i))),
        pl.BlockSpec((nb, CH, LANES), lambda b, i: (b, cidx(i), 0)),
        pl.BlockSpec((nb, LANES, CH), lambda b, i: (b, 0, cidx(i))),
        _const_spec((8, CH, CH)),
    ]
    args = [qkv, k_t, gates, gates_t, masks]
    if final:
        in_specs += [
            pl.BlockSpec((nb, CH, DN_W), lambda b, i: (b, cidx(i), 0)),
            pl.BlockSpec((nb, CH, DN_W), lambda b, i: (b, cidx(i), 0)),
            _const_spec((1, DN_DK)),
        ]
        args += [o_fwd, zz, norm_w]
    n = nb * DN_H
    return pl.pallas_call(
        functools.partial(_dn_kernel, rev=rev, final=final, nb=nb),
        grid=(bsz // nb, nc),
        in_specs=in_specs,
        out_specs=pl.BlockSpec((nb, CH, DN_W), lambda b, i: (b, cidx(i), 0)),
        out_shape=jax.ShapeDtypeStruct((bsz, t, DN_W), BF16 if final else F32),
        scratch_shapes=[pltpu.VMEM((n, DN_DK, DN_DK), F32)] + [pltpu.VMEM((n, CH, CH), F32)] * 2
        + [pltpu.VMEM((n, CH, CH), BF16)] * 4,
        compiler_params=pltpu.CompilerParams(dimension_semantics=("arbitrary", "arbitrary"),
                                             vmem_limit_bytes=VMEM_LIMIT),
        name="dn_bwd" if rev else "dn_fwd",
    )(*args)


def _ssd_kernel(*refs, rev, final, nb):
    if final:
        xbc_ref, g_ref, gt_ref, yf_ref, z_ref, d_ref, nw_ref, y_ref, s_ref = refs
    else:
        xbc_ref, g_ref, gt_ref, y_ref, s_ref = refs

    @pl.when(pl.program_id(1) == 0)
    def _():
        s_ref[...] = jnp.zeros_like(s_ref)

    r = lax.broadcasted_iota(jnp.int32, (CH, CH), 0)
    c = lax.broadcasted_iota(jnp.int32, (CH, CH), 1)
    incl = (r <= c) if rev else (r >= c)
    left = lax.broadcasted_iota(jnp.int32, (1, LANES), 1) < SSM_P
    lo = SSM_H if rev else 0
    last = 0 if rev else CH - 1
    pairs_per_group = SSM_H // SSM_G // 2
    n_pairs = SSM_H // 2

    def b_of(bi, g):
        return xbc_ref[bi, :, SSM_W + g * SSM_N:SSM_W + (g + 1) * SSM_N]

    def c16_of(bi, g):
        o = SSM_W + SSM_G * SSM_N
        return xbc_ref[bi, :, o + g * SSM_N:o + (g + 1) * SSM_N].astype(BF16)

    bts, cbs, yoffs = {}, {}, {}
    for bi in range(nb):
        for g in range(SSM_G):
            c16 = c16_of(bi, g)
            bts[bi, g] = b_of(bi, g).T
            cbs[bi, g] = _dot(c16, bts[bi, g].astype(BF16))
            for pp in range(pairs_per_group):
                p = g * pairs_per_group + pp
                yoffs[bi, p] = _dot(c16, s_ref[bi * n_pairs + p].astype(BF16))
    for bi in range(nb):
        for g in range(SSM_G):
            bt = bts[bi, g]
            cb = cbs[bi, g]
            ys = []
            for pp in range(pairs_per_group):
                p = g * pairs_per_group + pp
                ps = slice(p * LANES, (p + 1) * LANES)
                x = xbc_ref[bi, :, ps]
                d0 = G_DT + lo + 2 * p
                a0 = G_ACS + lo + 2 * p
                ac0b = jnp.broadcast_to(g_ref[bi, :, a0:a0 + 1], (CH, LANES))
                ac1b = jnp.broadcast_to(g_ref[bi, :, a0 + 1:a0 + 2], (CH, LANES))
                ar0, ar1 = gt_ref[bi, a0:a0 + 1, :], gt_ref[bi, a0 + 1:a0 + 2, :]
                dr0, dr1 = gt_ref[bi, d0:d0 + 1, :], gt_ref[bi, d0 + 1:d0 + 2, :]
                al0 = g_ref[bi, last:last + 1, a0:a0 + 1]
                al1 = g_ref[bi, last:last + 1, a0 + 1:a0 + 2]
                l0 = jnp.exp(jnp.where(incl, ac0b - ar0, -jnp.inf)) * dr0
                l1 = jnp.exp(jnp.where(incl, ac1b - ar1, -jnp.inf)) * dr1
                w0 = dr0 * jnp.exp(al0 - ar0)
                w1 = dr1 * jnp.exp(al1 - ar1)
                lhs = jnp.concatenate(
                    [jnp.concatenate([(cb * l0).astype(BF16), (cb * l1).astype(BF16)], axis=1),
                     jnp.concatenate([(bt * w0).astype(BF16), (bt * w1).astype(BF16)], axis=1)], axis=0)
                x01 = jnp.concatenate([jnp.where(left, x, 0.0).astype(BF16),
                                       jnp.where(left, 0.0, x).astype(BF16)], axis=0)
                prod = _dot(lhs, x01)
                acp = jnp.where(left, ac0b, ac1b)
                alp = jnp.where(left, al0, al1)
                y = prod[0:CH] + yoffs[bi, p] * jnp.exp(acp)
                si = bi * n_pairs + p
                s_ref[si] = s_ref[si] * jnp.exp(alp) + prod[CH:2 * CH]
                if final:
                    y = y + yf_ref[bi, :, ps] + d_ref[0:1, ps] * x
                    y = y * _silu(z_ref[bi, :, ps])
                ys.append((ps, y))
            if final:
                ssq = sum(jnp.sum(y * y, axis=-1, keepdims=True) for _, y in ys)
                inv = lax.rsqrt(ssq / (SSM_W // SSM_G) + RMS_EPS)
                for ps, y in ys:
                    y_ref[bi, :, ps] = (y * inv * nw_ref[0:1, ps]).astype(y_ref.dtype)
            else:
                for ps, y in ys:
                    y_ref[bi, :, ps] = y


def _ssd_call(xbc, gates, gates_t, rev, nb, y_fwd=None, zz=None, d_row=None, norm_w=None):
    bsz, t, _ = xbc.shape
    nc = t // CH
    final = y_fwd is not None

    def cidx(i):
        return (nc - 1 - i) if rev else i

    in_specs = [
        pl.BlockSpec((nb, CH, XBC_W), lambda b, i: (b, cidx(i), 0)),
        pl.BlockSpec((nb, CH, LANES), lambda b, i: (b, cidx(i), 0)),
        pl.BlockSpec((nb, LANES, CH), lambda b, i: (b, 0, cidx(i))),
    ]
    args = [xbc, gates, gates_t]
    if final:
        in_specs += [
            pl.BlockSpec((nb, CH, SSM_W), lambda b, i: (b, cidx(i), 0)),
            pl.BlockSpec((nb, CH, SSM_W), lambda b, i: (b, cidx(i), 1)),
            _const_spec((1, SSM_W)),
            _const_spec((1, SSM_W)),
        ]
        args += [y_fwd, zz, d_row, norm_w]
    return pl.pallas_call(
        functools.partial(_ssd_kernel, rev=rev, final=final, nb=nb),
        grid=(bsz // nb, nc),
        in_specs=in_specs,
        out_specs=pl.BlockSpec((nb, CH, SSM_W), lambda b, i: (b, cidx(i), 0)),
        out_shape=jax.ShapeDtypeStruct((bsz, t, SSM_W), BF16 if final else F32),
        scratch_shapes=[pltpu.VMEM((nb * SSM_H // 2, SSM_N, 2 * SSM_P), F32)],
        compiler_params=pltpu.CompilerParams(dimension_semantics=("arbitrary", "arbitrary"),
                                             vmem_limit_bytes=VMEM_LIMIT),
        name="ssd_bwd" if rev else "ssd_fwd",
    )(*args)


def _layer_norm(v, g, b):
    mu = jnp.mean(v, axis=-1, keepdims=True)
    d = v - mu
    var = jnp.mean(d * d, axis=-1, keepdims=True)
    return d * lax.rsqrt(var + LN_EPS) * g + b


def _tail_kernel(odn_ref, ossm_ref, x_ref, mod_ref, wout_ref, l1g_ref, l1b_ref, wg_ref, wu_ref, wd_ref,
                 l2g_ref, l2b_ref, out_ref, *, alpha, rows):
    gt1 = mod_ref[0, 2:3, :]
    sh2 = mod_ref[0, 3:4, :]
    sc2 = mod_ref[0, 4:5, :]
    gt2 = mod_ref[0, 5:6, :]
    parts = [slice(j * rows, (j + 1) * rows) for j in range(x_ref.shape[1] // rows)]
    mix = [_dot(odn_ref[0, rs, :], wout_ref[0:DN_W, :]) + _dot(ossm_ref[0, rs, :], wout_ref[DN_W:DN_W + SSM_W, :])
           for rs in parts]
    x1 = [_layer_norm(alpha * x_ref[0, rs, :] + gt1 * m, l1g_ref[...], l1b_ref[...]) for rs, m in zip(parts, mix)]
    h16 = [(v * (1.0 + sc2) + sh2).astype(BF16) for v in x1]
    act = [(_silu(_dot(h, wg_ref[...])) * _dot(h, wu_ref[...])).astype(BF16) for h in h16]
    ffn = [_dot(a, wd_ref[...]) for a in act]
    for rs, v, f in zip(parts, x1, ffn):
        out_ref[0, rs, :] = _layer_norm(alpha * v + gt2 * f, l2g_ref[...], l2b_ref[...])


def _tail_call(o_dn, o_ssm, x, mod, wout, l1g, l1b, wg, wu, wd, l2g, l2b, tb, rows, alpha):
    bsz, t, _ = x.shape
    nt = t // tb
    return pl.pallas_call(
        functools.partial(_tail_kernel, alpha=alpha, rows=rows),
        grid=(bsz, nt),
        in_specs=[
            pl.BlockSpec((1, tb, DN_W), lambda b, i: (b, i, 0)),
            pl.BlockSpec((1, tb, SSM_W), lambda b, i: (b, i, 0)),
            pl.BlockSpec((1, tb, D_MODEL), lambda b, i: (b, i, 0)),
            pl.BlockSpec((1, 6, D_MODEL), lambda b, i: (b, 0, 0)),
            _const_spec((DN_W + SSM_W, D_MODEL)),
            _const_spec((1, D_MODEL)),
            _const_spec((1, D_MODEL)),
            _const_spec((D_MODEL, D_FF)),
            _const_spec((D_MODEL, D_FF)),
            _const_spec((D_FF, D_MODEL)),
            _const_spec((1, D_MODEL)),
            _const_spec((1, D_MODEL)),
        ],
        out_specs=pl.BlockSpec((1, tb, D_MODEL), lambda b, i: (b, i, 0)),
        out_shape=jax.ShapeDtypeStruct((bsz, t, D_MODEL), F32),
        compiler_params=pltpu.CompilerParams(dimension_semantics=("arbitrary", "arbitrary"),
                                             vmem_limit_bytes=VMEM_LIMIT),
        name="tail",
    )(o_dn, o_ssm, x, mod, wout, l1g, l1b, wg, wu, wd, l2g, l2b)


def _prep_layer(w_in, dn_conv_w, dn_A_log, dn_dt_bias, dn_norm_w, ssm_conv_w, ssm_conv_b, ssm_A_log,
                ssm_dt_bias, ssm_D, ssm_norm_w, w_out, ln1_g, ln1_b, w_gate, w_up, w_down, ln2_g, ln2_b):
    o = 0
    w_qkv = w_in[:, o:o + 3 * DN_W]; o += 3 * DN_W
    w_dz = w_in[:, o:o + DN_W]; o += DN_W
    w_a = w_in[:, o:o + 2 * DN_H]; o += 2 * DN_H
    w_b = w_in[:, o:o + 2 * DN_H]; o += 2 * DN_H
    w_sz = w_in[:, o:o + SSM_W]; o += SSM_W
    w_xbc = w_in[:, o:o + XBC_W]; o += XBC_W
    w_dt = w_in[:, o:o + 2 * SSM_H]
    wbig = jnp.concatenate([w_qkv, w_xbc], axis=1).astype(BF16)
    pad = jnp.zeros((D_MODEL, LANES - G_ACS - 2 * SSM_H), F32)
    wzs = jnp.concatenate([w_dz, w_sz, w_a, w_b, w_dt, w_dt, pad], axis=1).astype(BF16)
    zrow = lambda n: jnp.zeros((n,), F32)
    brow = jnp.concatenate([dn_dt_bias.reshape(-1), zrow(2 * DN_H), ssm_dt_bias.reshape(-1),
                            ssm_dt_bias.reshape(-1), zrow(LANES - G_ACS - 2 * SSM_H)]).reshape(1, LANES)
    arow = jnp.concatenate([-jnp.exp(dn_A_log.reshape(-1)), zrow(2 * DN_H + 2 * SSM_H),
                            -jnp.exp(ssm_A_log.reshape(-1)), zrow(LANES - G_ACS - 2 * SSM_H)]).reshape(1, LANES)
    cw = jnp.concatenate([dn_conv_w, ssm_conv_w], axis=1)
    cb = jnp.concatenate([zrow(3 * DN_W), ssm_conv_b]).reshape(1, CONV_W)
    return dict(
        wbig=wbig, wzs=wzs, brow=brow, arow=arow, cw=cw, cb=cb,
        dn_nw=dn_norm_w.reshape(1, DN_DK), d_row=jnp.repeat(ssm_D, SSM_P).reshape(1, SSM_W),
        ssm_nw=ssm_norm_w.reshape(1, SSM_W), wout=w_out.astype(BF16),
        l1g=ln1_g.reshape(1, D_MODEL), l1b=ln1_b.reshape(1, D_MODEL),
        wg=w_gate.astype(BF16), wu=w_up.astype(BF16), wd=w_down.astype(BF16),
        l2g=ln2_g.reshape(1, D_MODEL), l2b=ln2_b.reshape(1, D_MODEL))


def _encoder_layer(x, mod, p, alpha):
    qkv, k_t, xbc, zz, gates, gates_t = _inproj_call(x, mod, p["wbig"], p["wzs"], p["cw"], p["cb"],
                                                p["brow"], p["arow"], tb=256)
    nb = math.gcd(x.shape[0], 4)
    masks = _dn_masks()
    o_f = _dn_call(qkv, k_t, gates, gates_t, masks, rev=False, nb=nb)
    o_dn = _dn_call(qkv, k_t, gates, gates_t, masks, rev=True, nb=nb, o_fwd=o_f, zz=zz, norm_w=p["dn_nw"])
    y_f = _ssd_call(xbc, gates, gates_t, rev=False, nb=nb)
    o_ssm = _ssd_call(xbc, gates, gates_t, rev=True, nb=nb, y_fwd=y_f, zz=zz, d_row=p["d_row"],
                      norm_w=p["ssm_nw"])
    return _tail_call(o_dn, o_ssm, x, mod, p["wout"], p["l1g"], p["l1b"], p["wg"], p["wu"], p["wd"],
                      p["l2g"], p["l2b"], tb=512, rows=256, alpha=alpha)


def kernel(x_prompt, x_sample, c_prompt, c_sample, w_ada, b_ada, w_in, dn_conv_w, dn_A_log, dn_dt_bias, dn_norm_w, ssm_conv_w, ssm_conv_b, ssm_A_log, ssm_dt_bias, ssm_D, ssm_norm_w, w_out, ln1_g, ln1_b, w_gate, w_up, w_down, ln2_g, ln2_b):
    depth = w_ada.shape[0]
    alpha = (2 * depth) ** 0.25
    layer_params = (w_in, dn_conv_w, dn_A_log, dn_dt_bias, dn_norm_w, ssm_conv_w, ssm_conv_b, ssm_A_log,
                    ssm_dt_bias, ssm_D, ssm_norm_w, w_out, ln1_g, ln1_b, w_gate, w_up, w_down, ln2_g, ln2_b)
    nb_p, nb_s = c_prompt.shape[0], c_sample.shape[0]
    c_all = jnp.concatenate([c_prompt, c_sample], axis=0)
    nb = nb_p + nb_s
    c_pad = jnp.pad(c_all, ((0, (-nb) % 8), (0, 0)))
    xs = [x_prompt, x_sample]
    for l in range(depth):
        p = _prep_layer(*[w[l] for w in layer_params])
        mod = _mod_call(c_pad, w_ada[l], b_ada[l].reshape(1, -1))[:nb].reshape(nb, 6, D_MODEL)
        xs = [_encoder_layer(xs[0], mod[:nb_p], p, alpha), _encoder_layer(xs[1], mod[nb_p:], p, alpha)]
    return (xs[0], xs[1])
```

```python
import functools
import math

import jax
import jax.numpy as jnp
from jax import lax
from jax.experimental import pallas as pl
from jax.experimental.pallas import tpu as pltpu

F32 = jnp.float32
BF16 = jnp.bfloat16
HIGHEST = lax.Precision.HIGHEST

D_MODEL = 1024
DN_H = 4
DN_DK = 128
DN_W = DN_H * DN_DK
SSM_H = 8
SSM_P = 64
SSM_W = SSM_H * SSM_P
SSM_G = 2
SSM_N = 128
XBC_W = SSM_W + 2 * SSM_G * SSM_N
CONV_K = 5
D_FF = 2816
CONV_W = 3 * DN_W + XBC_W
LN_EPS = 1e-5
RMS_EPS = 1e-6

CH = 128
HALO = 8
LANES = 128

G_GC = 0
G_BETA = 8
G_DT = 16
G_ACS = 32

VMEM_LIMIT = 56 * 1024 * 1024


def _sigmoid(v):
    return 1.0 / (1.0 + jnp.exp(-v))


def _silu(v):
    return v * _sigmoid(v)


def _softplus(v):
    return jnp.maximum(v, 0.0) + jnp.log1p(jnp.exp(-jnp.abs(v)))


def _dot(a, b):
    return jnp.dot(a, b, preferred_element_type=F32)


def _dot_nt(a, b):
    return lax.dot_general(a, b, (((1,), (1,)), ((), ())), preferred_element_type=F32)


def _dot_tn(a, b):
    return lax.dot_general(a, b, (((0,), (0,)), ((), ())), preferred_element_type=F32)


def _const_spec(shape):
    nd = len(shape)
    return pl.BlockSpec(shape, lambda *_: (0,) * nd, pipeline_mode=pl.Buffered(1))


def _mod_kernel(c_ref, w_ref, b_ref, o_ref):
    s = _silu(c_ref[...])
    o_ref[...] = jnp.dot(s, w_ref[...], precision=HIGHEST, preferred_element_type=F32) + b_ref[...]


def _mod_call(c_pad, w_ada, b_ada):
    bp = c_pad.shape[0]
    n = w_ada.shape[1]
    blk = 1024
    return pl.pallas_call(
        _mod_kernel,
        grid=(n // blk,),
        in_specs=[pl.BlockSpec((bp, D_MODEL), lambda j: (0, 0)),
                  pl.BlockSpec((D_MODEL, blk), lambda j: (0, j)),
                  pl.BlockSpec((1, blk), lambda j: (0, j))],
        out_specs=pl.BlockSpec((bp, blk), lambda j: (0, j)),
        out_shape=jax.ShapeDtypeStruct((bp, n), F32),
        compiler_params=pltpu.CompilerParams(dimension_semantics=("arbitrary",), vmem_limit_bytes=VMEM_LIMIT),
        name="mod",
    )(c_pad, w_ada, b_ada)


def _split3(v):
    hi = v.astype(BF16)
    r1 = v - hi.astype(F32)
    mid = r1.astype(BF16)
    lo = (r1 - mid.astype(F32)).astype(BF16)
    return hi, mid, lo


def _inproj_kernel(xm_ref, xp_ref, xn_ref, mod_ref, wbig_ref, wzs_ref, cw_ref, cb_ref, brow_ref, arow_ref,
                   qkv_ref, kt_ref, xbc_ref, zz_ref, g_ref, gt_ref, hbuf, pbuf, ybuf, *, tb, nt):
    i = pl.program_id(1)
    sh = mod_ref[0, 0:1, :]
    sc = mod_ref[0, 1:2, :]

    def modulate(v):
        return v * (1.0 + sc) + sh

    keep_prev = jnp.where(i > 0, 1.0, 0.0)
    keep_next = jnp.where(i < nt - 1, 1.0, 0.0)
    hp = modulate(xp_ref[0]) * keep_prev
    hm = modulate(xm_ref[0])
    hn = modulate(xn_ref[0]) * keep_next
    hm16 = hm.astype(BF16)

    nv = (tb + 2 * HALO) // 8
    for lg in range(D_MODEL // LANES):
        ls = slice(lg * LANES, (lg + 1) * LANES)
        hbuf[lg, 0:HALO, :] = hp[:, ls]
        hbuf[lg, HALO:HALO + tb, :] = hm[:, ls]
        hbuf[lg, HALO + tb:2 * HALO + tb, :] = hn[:, ls]
    h_perm = jnp.concatenate(
        [jnp.concatenate([hbuf[lg, pl.ds(v, 8, stride=nv), :] for lg in range(D_MODEL // LANES)], axis=1)
         for v in range(nv)], axis=0).astype(BF16)

    gw = 2 * LANES
    n_groups = CONV_W // gw
    z_groups = 2 * DN_W // gw

    def project(gi):
        pbuf[:, gi * gw:(gi + 1) * gw] = _dot(h_perm, wbig_ref[:, gi * gw:(gi + 1) * gw])

    def tile(cs, u):
        if u < 0:
            return pltpu.roll(pbuf[(u + nv) * 8:(u + nv + 1) * 8, cs], 1, axis=0)
        if u >= nv:
            return pltpu.roll(pbuf[(u - nv) * 8:(u - nv + 1) * 8, cs], 7, axis=0)
        return pbuf[u * 8:(u + 1) * 8, cs]

    def conv(cg):
        cs = slice(cg * LANES, (cg + 1) * LANES)
        w = [cw_ref[j:j + 1, cs] for j in range(CONV_K)]
        bias = cb_ref[0:1, cs]
        for v in range(nv):
            acc = bias + tile(cs, v - CONV_K // 2) * w[0]
            for j in range(1, CONV_K):
                acc = acc + tile(cs, v + j - CONV_K // 2) * w[j]
            y = _silu(acc)
            if cg < 2 * DN_H:
                y = y * lax.rsqrt(jnp.sum(y * y, axis=-1, keepdims=True) + RMS_EPS)
                if cg < DN_H:
                    y = y * (DN_DK ** -0.5)
            ybuf[cg, pl.ds(v, 8, stride=nv), :] = y

    def emit(cg):
        y = ybuf[cg, HALO:HALO + tb, :]
        if cg < 3 * DN_H:
            qkv_ref[0, :, cg * LANES:(cg + 1) * LANES] = y.astype(BF16)
        else:
            c2 = cg - 3 * DN_H
            xbc_ref[0, :, c2 * LANES:(c2 + 1) * LANES] = y.astype(BF16)
        if DN_H <= cg < 2 * DN_H:
            for ck in range(tb // CH):
                kt_ref[0, ck, (cg - DN_H) * DN_DK:(cg - DN_H + 1) * DN_DK, :] = (
                    y[ck * CH:(ck + 1) * CH, :].T.astype(BF16))

    project(0)
    for gi in range(n_groups):
        if gi + 1 < n_groups:
            project(gi + 1)
        if gi < z_groups:
            zz_ref[0, :, gi * gw:(gi + 1) * gw] = _dot(hm16, wzs_ref[:, gi * gw:(gi + 1) * gw]).astype(BF16)
        for cg in range(gi * gw // LANES, (gi + 1) * gw // LANES):
            conv(cg)
            emit(cg)

    ps = _dot(hm16, wzs_ref[:, 2 * DN_W:2 * DN_W + LANES])
    lane = lax.broadcasted_iota(jnp.int32, (1, LANES), 1)
    sp = _softplus(ps + brow_ref[...])
    decay_src = sp * arow_ref[...]
    is_beta = (lane >= G_BETA) & (lane < G_DT)
    is_dt = (lane >= G_DT) & (lane < G_ACS)
    raw = jnp.where(is_beta, _sigmoid(ps), jnp.where(is_dt, sp, 0.0))
    fwd_cum = (lane < G_GC + DN_H) | ((lane >= G_ACS) & (lane < G_ACS + SSM_H))
    bwd_cum = ((lane >= G_GC + DN_H) & (lane < G_BETA)) | ((lane >= G_ACS + SSM_H) & (lane < G_ACS + 2 * SSM_H))
    r = lax.broadcasted_iota(jnp.int32, (CH, 2 * CH), 0)
    c = lax.broadcasted_iota(jnp.int32, (CH, 2 * CH), 1)
    tri2 = jnp.where(((c < CH) & (r >= c)) | ((c >= CH) & (r <= c - CH)), 1.0, 0.0).astype(BF16)
    for ck in range(tb // CH):
        rs = slice(ck * CH, (ck + 1) * CH)
        src = decay_src[rs]
        stacked = jnp.concatenate([jnp.where(fwd_cum, src, 0.0), jnp.where(bwd_cum, src, 0.0)], axis=0)
        hi, mid, lo = _split3(stacked)
        cum = _dot(tri2, hi) + _dot(tri2, mid) + _dot(tri2, lo)
        gates = jnp.where(fwd_cum | bwd_cum, cum, raw[rs])
        g_ref[0, rs, :] = gates
        gt_ref[0, ck] = gates.T


def _inproj_call(x, mod, wbig, wzs, cw, cb, brow, arow, tb):
    bsz, t, _ = x.shape
    nt = t // tb
    hb = tb // HALO
    nhb = t // HALO
    kern = functools.partial(_inproj_kernel, tb=tb, nt=nt)
    return pl.pallas_call(
        kern,
        grid=(bsz, nt),
        in_specs=[
            pl.BlockSpec((1, tb, D_MODEL), lambda b, i: (b, i, 0)),
            pl.BlockSpec((1, HALO, D_MODEL), lambda b, i: (b, jnp.maximum(i * hb - 1, 0), 0)),
            pl.BlockSpec((1, HALO, D_MODEL), lambda b, i: (b, jnp.minimum((i + 1) * hb, nhb - 1), 0)),
            pl.BlockSpec((1, 6, D_MODEL), lambda b, i: (b, 0, 0)),
            _const_spec((D_MODEL, CONV_W)),
            _const_spec((D_MODEL, 2 * DN_W + LANES)),
            _const_spec((CONV_K, CONV_W)),
            _const_spec((1, CONV_W)),
            _const_spec((1, LANES)),
            _const_spec((1, LANES)),
        ],
        out_specs=[
            pl.BlockSpec((1, tb, 3 * DN_W), lambda b, i: (b, i, 0)),
            pl.BlockSpec((1, tb // CH, DN_W, CH), lambda b, i: (b, i, 0, 0)),
            pl.BlockSpec((1, tb, XBC_W), lambda b, i: (b, i, 0)),
            pl.BlockSpec((1, tb, 2 * DN_W), lambda b, i: (b, i, 0)),
            pl.BlockSpec((1, tb, LANES), lambda b, i: (b, i, 0)),
            pl.BlockSpec((1, tb // CH, LANES, CH), lambda b, i: (b, i, 0, 0)),
        ],
        out_shape=[
            jax.ShapeDtypeStruct((bsz, t, 3 * DN_W), BF16),
            jax.ShapeDtypeStruct((bsz, t // CH, DN_W, CH), BF16),
            jax.ShapeDtypeStruct((bsz, t, XBC_W), BF16),
            jax.ShapeDtypeStruct((bsz, t, 2 * DN_W), BF16),
            jax.ShapeDtypeStruct((bsz, t, LANES), F32),
            jax.ShapeDtypeStruct((bsz, t // CH, LANES, CH), F32),
        ],
        scratch_shapes=[pltpu.VMEM((D_MODEL // LANES, tb + 2 * HALO, LANES), F32),
                        pltpu.VMEM((tb + 2 * HALO, CONV_W), F32),
                        pltpu.VMEM((CONV_W // LANES, tb + 2 * HALO, LANES), F32)],
        compiler_params=pltpu.CompilerParams(dimension_semantics=("arbitrary", "arbitrary"),
                                             vmem_limit_bytes=VMEM_LIMIT),
        name="inproj",
    )(x, x, x, mod, wbig, wzs, cw, cb, brow, arow)


def _dn_masks():
    r = jnp.arange(CH)[:, None]
    c = jnp.arange(CH)[None, :]
    ms = [(r >> 1) == (c >> 1)]
    for s in range(1, 7):
        ms.append(((r >> (s + 1)) == (c >> (s + 1))) & ((r >> s) != (c >> s)))
    ms.append(r == c)
    return jnp.stack(ms).astype(BF16)


def _dn_stages(refs, rev, final, nb):
    if final:
        qkv_ref, kt_ref, g_ref, gt_ref, m_ref, of_ref, z_ref, nw_ref, o_ref, s_ref, gb_s, bb_s, l16, x16, t16, a16 = refs
    else:
        qkv_ref, kt_ref, g_ref, gt_ref, m_ref, o_ref, s_ref, gb_s, bb_s, l16, x16, t16, a16 = refs

    r = lax.broadcasted_iota(jnp.int32, (CH, CH), 0)
    c = lax.broadcasted_iota(jnp.int32, (CH, CH), 1)
    incl = (r <= c) if rev else (r >= c)
    strict = (r < c) if rev else (r > c)
    lo = DN_H if rev else 0
    last = 0 if rev else CH - 1
    chains = [(bi, h) for bi in range(nb) for h in range(DN_H)]
    n = len(chains)

    def q_of(bi, h):
        return qkv_ref[bi, :, h * DN_DK:(h + 1) * DN_DK]

    def k_of(bi, h):
        return qkv_ref[bi, :, DN_W + h * DN_DK:DN_W + (h + 1) * DN_DK]

    def v_of(bi, h):
        return qkv_ref[bi, :, 2 * DN_W + h * DN_DK:2 * DN_W + (h + 1) * DN_DK]

    def kt_of(bi, h):
        return kt_ref[bi, 0, h * DN_DK:(h + 1) * DN_DK, :]

    def gcc_of(bi, h):
        return g_ref[bi, :, G_GC + lo + h:G_GC + lo + h + 1]

    def beta_of(bi, h):
        return g_ref[bi, :, G_BETA + lo + h:G_BETA + lo + h + 1]

    def stage_a():
        for ci, (bi, h) in enumerate(chains):
            gi = G_GC + lo + h
            gb = jnp.broadcast_to(gcc_of(bi, h), (CH, CH))
            bb = jnp.broadcast_to(beta_of(bi, h), (CH, CH))
            gb_s[ci] = gb
            bb_s[ci] = bb
            decay = jnp.exp(jnp.where(incl, gb - gt_ref[bi, 0, gi:gi + 1, :], -jnp.inf))
            kt16 = kt_of(bi, h)
            lm = jnp.where(strict, _dot((k_of(bi, h) * bb).astype(BF16), kt16) * decay, 0.0).astype(BF16)
            l16[ci] = lm
            x16[ci] = m_ref[7] - lm * m_ref[0]
            a16[ci] = (_dot(q_of(bi, h).astype(BF16), kt16) * decay).astype(BF16)

    def level(s):
        for ci in range(n):
            t16[ci] = _dot(l16[ci] * m_ref[s], x16[ci]).astype(BF16)
        for ci in range(n):
            x16[ci] = x16[ci] - _dot(x16[ci], t16[ci]).astype(BF16)

    def stage_c():
        for ci, (bi, h) in enumerate(chains):
            kg16 = (k_of(bi, h) * jnp.exp(gb_s[ci])).astype(BF16)
            resid = v_of(bi, h) - _dot(kg16, s_ref[ci].astype(BF16))
            t16[ci] = (bb_s[ci] * resid).astype(BF16)
        for ci in range(n):
            l16[ci] = _dot(x16[ci], t16[ci]).astype(BF16)
        for ci, (bi, h) in enumerate(chains):
            hs = slice(h * DN_DK, (h + 1) * DN_DK)
            gi = G_GC + lo + h
            glast = g_ref[bi, last:last + 1, gi:gi + 1]
            st = s_ref[ci]
            vnew16 = l16[ci]
            qg16 = (q_of(bi, h) * jnp.exp(gb_s[ci])).astype(BF16)
            o = _dot(jnp.concatenate([qg16, a16[ci]], axis=1), jnp.concatenate([st.astype(BF16), vnew16], axis=0))
            kdec_t16 = (kt_of(bi, h) * jnp.exp(glast - gt_ref[bi, 0, gi:gi + 1, :])).astype(BF16)
            s_ref[ci] = st * jnp.exp(glast) + _dot(kdec_t16, vnew16)
            if final:
                o = o + of_ref[bi, :, hs]
                o = o * lax.rsqrt(jnp.mean(o * o, axis=-1, keepdims=True) + RMS_EPS)
                o = o * nw_ref[...] * _silu(z_ref[bi, :, hs].astype(F32))
            o_ref[bi, :, hs] = o.astype(o_ref.dtype)

    return stage_a, level, stage_c


def _ssd_stages(refs, rev, final, nb):
    if final:
        xbc_ref, g_ref, gt_ref, yf_ref, z_ref, d_ref, nw_ref, y_ref, s_ref = refs
    else:
        xbc_ref, g_ref, gt_ref, y_ref, s_ref = refs

    r = lax.broadcasted_iota(jnp.int32, (CH, CH), 0)
    c = lax.broadcasted_iota(jnp.int32, (CH, CH), 1)
    incl = (r <= c) if rev else (r >= c)
    left = lax.broadcasted_iota(jnp.int32, (1, LANES), 1) < SSM_P
    lo = SSM_H if rev else 0
    last = 0 if rev else CH - 1
    pairs_per_group = SSM_H // SSM_G // 2
    n_pairs = SSM_H // 2

    def b_of(bi, g):
        return xbc_ref[bi, :, SSM_W + g * SSM_N:SSM_W + (g + 1) * SSM_N]

    def c16_of(bi, g):
        o = SSM_W + SSM_G * SSM_N
        return xbc_ref[bi, :, o + g * SSM_N:o + (g + 1) * SSM_N].astype(BF16)

    bts, cbs, yoffs = {}, {}, {}

    def stage1():
        for bi in range(nb):
            for g in range(SSM_G):
                c16 = c16_of(bi, g)
                bts[bi, g] = b_of(bi, g).astype(F32).T
                cbs[bi, g] = _dot(c16, bts[bi, g].astype(BF16))
                for pp in range(pairs_per_group):
                    p = g * pairs_per_group + pp
                    yoffs[bi, p] = _dot(c16, s_ref[bi * n_pairs + p].astype(BF16))

    def group(bi, g):
        bt = bts[bi, g]
        cb = cbs[bi, g]
        ys = []
        for pp in range(pairs_per_group):
            p = g * pairs_per_group + pp
            ps = slice(p * LANES, (p + 1) * LANES)
            x = xbc_ref[bi, :, ps].astype(F32)
            d0 = G_DT + lo + 2 * p
            a0 = G_ACS + lo + 2 * p
            ac0b = jnp.broadcast_to(g_ref[bi, :, a0:a0 + 1], (CH, LANES))
            ac1b = jnp.broadcast_to(g_ref[bi, :, a0 + 1:a0 + 2], (CH, LANES))
            ar0, ar1 = gt_ref[bi, 0, a0:a0 + 1, :], gt_ref[bi, 0, a0 + 1:a0 + 2, :]
            dr0, dr1 = gt_ref[bi, 0, d0:d0 + 1, :], gt_ref[bi, 0, d0 + 1:d0 + 2, :]
            al0 = g_ref[bi, last:last + 1, a0:a0 + 1]
            al1 = g_ref[bi, last:last + 1, a0 + 1:a0 + 2]
            l0 = jnp.exp(jnp.where(incl, ac0b - ar0, -jnp.inf)) * dr0
            l1 = jnp.exp(jnp.where(incl, ac1b - ar1, -jnp.inf)) * dr1
            w0 = dr0 * jnp.exp(al0 - ar0)
            w1 = dr1 * jnp.exp(al1 - ar1)
            lhs = jnp.concatenate(
                [jnp.concatenate([(cb * l0).astype(BF16), (cb * l1).astype(BF16)], axis=1),
                 jnp.concatenate([(bt * w0).astype(BF16), (bt * w1).astype(BF16)], axis=1)], axis=0)
            x01 = jnp.concatenate([jnp.where(left, x, 0.0).astype(BF16),
                                   jnp.where(left, 0.0, x).astype(BF16)], axis=0)
            prod = _dot(lhs, x01)
            acp = jnp.where(left, ac0b, ac1b)
            alp = jnp.where(left, al0, al1)
            y = prod[0:CH] + yoffs[bi, p] * jnp.exp(acp)
            si = bi * n_pairs + p
            s_ref[si] = s_ref[si] * jnp.exp(alp) + prod[CH:2 * CH]
            if final:
                y = y + yf_ref[bi, :, ps] + d_ref[0:1, ps] * x
                y = y * _silu(z_ref[bi, :, ps].astype(F32))
            ys.append((ps, y))
        if final:
            ssq = sum(jnp.sum(y * y, axis=-1, keepdims=True) for _, y in ys)
            inv = lax.rsqrt(ssq / (SSM_W // SSM_G) + RMS_EPS)
            for ps, y in ys:
                y_ref[bi, :, ps] = (y * inv * nw_ref[0:1, ps]).astype(y_ref.dtype)
        else:
            for ps, y in ys:
                y_ref[bi, :, ps] = y.astype(y_ref.dtype)

    return stage1, group


def _mixer_kernel(*refs, rev, final, nb):
    if final:
        (qkv_ref, kt_ref, xbc_ref, g_ref, gt_ref, m_ref, of_ref, yf_ref, zd_ref, zs_ref, dnw_ref, d_ref, snw_ref,
         o_ref, y_ref, sd_ref, gb_s, bb_s, l16, x16, t16, a16, ss_ref) = refs
        dn_refs = (qkv_ref, kt_ref, g_ref, gt_ref, m_ref, of_ref, zd_ref, dnw_ref, o_ref, sd_ref,
                   gb_s, bb_s, l16, x16, t16, a16)
        ssd_refs = (xbc_ref, g_ref, gt_ref, yf_ref, zs_ref, d_ref, snw_ref, y_ref, ss_ref)
    else:
        (qkv_ref, kt_ref, xbc_ref, g_ref, gt_ref, m_ref, o_ref, y_ref,
         sd_ref, gb_s, bb_s, l16, x16, t16, a16, ss_ref) = refs
        dn_refs = (qkv_ref, kt_ref, g_ref, gt_ref, m_ref, o_ref, sd_ref, gb_s, bb_s, l16, x16, t16, a16)
        ssd_refs = (xbc_ref, g_ref, gt_ref, y_ref, ss_ref)

    @pl.when(pl.program_id(1) == 0)
    def _():
        sd_ref[...] = jnp.zeros_like(sd_ref)
        ss_ref[...] = jnp.zeros_like(ss_ref)

    dn_a, dn_level, dn_c = _dn_stages(dn_refs, rev, final, nb)
    ssd_1, ssd_group = _ssd_stages(ssd_refs, rev, final, nb)
    groups = [(bi, g) for bi in range(nb) for g in range(SSM_G)]
    n_levels = 6
    dn_a()
    ssd_1()
    done = 0
    for s in range(1, n_levels + 1):
        dn_level(s)
        upto = len(groups) * s // n_levels
        for bi, g in groups[done:upto]:
            ssd_group(bi, g)
        done = upto
    dn_c()


def _mixer_call(qkv, k_t, xbc, gates, gates_t, masks, rev, nb, o_fwd=None, y_fwd=None, zz=None, dn_nw=None,
                d_row=None, ssm_nw=None):
    bsz, t, _ = qkv.shape
    nc = t // CH
    final = o_fwd is not None

    def cidx(i):
        return (nc - 1 - i) if rev else i

    def tok(width, col=0):
        return pl.BlockSpec((nb, CH, width), lambda b, i: (b, cidx(i), col))

    def per_chunk(rows):
        return pl.BlockSpec((nb, 1, rows, CH), lambda b, i: (b, cidx(i), 0, 0))

    in_specs = [tok(3 * DN_W), per_chunk(DN_W), tok(XBC_W), tok(LANES), per_chunk(LANES), _const_spec((8, CH, CH))]
    args = [qkv, k_t, xbc, gates, gates_t, masks]
    if final:
        in_specs += [tok(DN_W), tok(SSM_W), tok(DN_W, 0), tok(SSM_W, 1),
                     _const_spec((1, DN_DK)), _const_spec((1, SSM_W)), _const_spec((1, SSM_W))]
        args += [o_fwd, y_fwd, zz, zz, dn_nw, d_row, ssm_nw]
    n = nb * DN_H
    return pl.pallas_call(
        functools.partial(_mixer_kernel, rev=rev, final=final, nb=nb),
        grid=(bsz // nb, nc),
        in_specs=in_specs,
        out_specs=[tok(DN_W), tok(SSM_W)],
        out_shape=[jax.ShapeDtypeStruct((bsz, t, DN_W), BF16), jax.ShapeDtypeStruct((bsz, t, SSM_W), BF16)],
        scratch_shapes=[pltpu.VMEM((n, DN_DK, DN_DK), F32)] + [pltpu.VMEM((n, CH, CH), F32)] * 2
        + [pltpu.VMEM((n, CH, CH), BF16)] * 4 + [pltpu.VMEM((nb * SSM_H // 2, SSM_N, 2 * SSM_P), F32)],
        compiler_params=pltpu.CompilerParams(dimension_semantics=("arbitrary", "arbitrary"),
                                             vmem_limit_bytes=VMEM_LIMIT),
        name="mixer_bwd" if rev else "mixer_fwd",
    )(*args)


def _layer_norm(v, g, b):
    mu = jnp.mean(v, axis=-1, keepdims=True)
    d = v - mu
    var = jnp.mean(d * d, axis=-1, keepdims=True)
    return d * lax.rsqrt(var + LN_EPS) * g + b


def _tail_kernel(odn_ref, ossm_ref, x_ref, mod_ref, wout_ref, l1g_ref, l1b_ref, wg_ref, wu_ref, wd_ref,
                 l2g_ref, l2b_ref, out_ref, *, alpha, rows):
    gt1 = mod_ref[0, 2:3, :]
    sh2 = mod_ref[0, 3:4, :]
    sc2 = mod_ref[0, 4:5, :]
    gt2 = mod_ref[0, 5:6, :]
    parts = [slice(j * rows, (j + 1) * rows) for j in range(x_ref.shape[1] // rows)]
    mix = [_dot(odn_ref[0, rs, :], wout_ref[0:DN_W, :]) + _dot(ossm_ref[0, rs, :], wout_ref[DN_W:DN_W + SSM_W, :])
           for rs in parts]
    x1 = [_layer_norm(alpha * x_ref[0, rs, :] + gt1 * m, l1g_ref[...], l1b_ref[...]) for rs, m in zip(parts, mix)]
    h16 = [(v * (1.0 + sc2) + sh2).astype(BF16) for v in x1]
    act = [(_silu(_dot(h, wg_ref[...])) * _dot(h, wu_ref[...])).astype(BF16) for h in h16]
    ffn = [_dot(a, wd_ref[...]) for a in act]
    for rs, v, f in zip(parts, x1, ffn):
        out_ref[0, rs, :] = _layer_norm(alpha * v + gt2 * f, l2g_ref[...], l2b_ref[...])


def _tail_call(o_dn, o_ssm, x, mod, wout, l1g, l1b, wg, wu, wd, l2g, l2b, tb, rows, alpha):
    bsz, t, _ = x.shape
    nt = t // tb
    return pl.pallas_call(
        functools.partial(_tail_kernel, alpha=alpha, rows=rows),
        grid=(bsz, nt),
        in_specs=[
            pl.BlockSpec((1, tb, DN_W), lambda b, i: (b, i, 0)),
            pl.BlockSpec((1, tb, SSM_W), lambda b, i: (b, i, 0)),
            pl.BlockSpec((1, tb, D_MODEL), lambda b, i: (b, i, 0)),
            pl.BlockSpec((1, 6, D_MODEL), lambda b, i: (b, 0, 0)),
            _const_spec((DN_W + SSM_W, D_MODEL)),
            _const_spec((1, D_MODEL)),
            _const_spec((1, D_MODEL)),
            _const_spec((D_MODEL, D_FF)),
            _const_spec((D_MODEL, D_FF)),
            _const_spec((D_FF, D_MODEL)),
            _const_spec((1, D_MODEL)),
            _const_spec((1, D_MODEL)),
        ],
        out_specs=pl.BlockSpec((1, tb, D_MODEL), lambda b, i: (b, i, 0)),
        out_shape=jax.ShapeDtypeStruct((bsz, t, D_MODEL), F32),
        compiler_params=pltpu.CompilerParams(dimension_semantics=("arbitrary", "arbitrary"),
                                             vmem_limit_bytes=VMEM_LIMIT),
        name="tail",
    )(o_dn, o_ssm, x, mod, wout, l1g, l1b, wg, wu, wd, l2g, l2b)


def _prep_layer(w_in, dn_conv_w, dn_A_log, dn_dt_bias, dn_norm_w, ssm_conv_w, ssm_conv_b, ssm_A_log,
                ssm_dt_bias, ssm_D, ssm_norm_w, w_out, ln1_g, ln1_b, w_gate, w_up, w_down, ln2_g, ln2_b):
    o = 0
    w_qkv = w_in[:, o:o + 3 * DN_W]; o += 3 * DN_W
    w_dz = w_in[:, o:o + DN_W]; o += DN_W
    w_a = w_in[:, o:o + 2 * DN_H]; o += 2 * DN_H
    w_b = w_in[:, o:o + 2 * DN_H]; o += 2 * DN_H
    w_sz = w_in[:, o:o + SSM_W]; o += SSM_W
    w_xbc = w_in[:, o:o + XBC_W]; o += XBC_W
    w_dt = w_in[:, o:o + 2 * SSM_H]
    wbig = jnp.concatenate([w_qkv, w_xbc], axis=1).astype(BF16)
    pad = jnp.zeros((D_MODEL, LANES - G_ACS - 2 * SSM_H), F32)
    wzs = jnp.concatenate([w_dz, w_sz, w_a, w_b, w_dt, w_dt, pad], axis=1).astype(BF16)
    zrow = lambda n: jnp.zeros((n,), F32)
    brow = jnp.concatenate([dn_dt_bias.reshape(-1), zrow(2 * DN_H), ssm_dt_bias.reshape(-1),
                            ssm_dt_bias.reshape(-1), zrow(LANES - G_ACS - 2 * SSM_H)]).reshape(1, LANES)
    arow = jnp.concatenate([-jnp.exp(dn_A_log.reshape(-1)), zrow(2 * DN_H + 2 * SSM_H),
                            -jnp.exp(ssm_A_log.reshape(-1)), zrow(LANES - G_ACS - 2 * SSM_H)]).reshape(1, LANES)
    cw = jnp.concatenate([dn_conv_w, ssm_conv_w], axis=1)
    cb = jnp.concatenate([zrow(3 * DN_W), ssm_conv_b]).reshape(1, CONV_W)
    return dict(
        wbig=wbig, wzs=wzs, brow=brow, arow=arow, cw=cw, cb=cb,
        dn_nw=dn_norm_w.reshape(1, DN_DK), d_row=jnp.repeat(ssm_D, SSM_P).reshape(1, SSM_W),
        ssm_nw=ssm_norm_w.reshape(1, SSM_W), wout=w_out.astype(BF16),
        l1g=ln1_g.reshape(1, D_MODEL), l1b=ln1_b.reshape(1, D_MODEL),
        wg=w_gate.astype(BF16), wu=w_up.astype(BF16), wd=w_down.astype(BF16),
        l2g=ln2_g.reshape(1, D_MODEL), l2b=ln2_b.reshape(1, D_MODEL))


def _encoder_layer(x, mod, p, alpha):
    qkv, k_t, xbc, zz, gates, gates_t = _inproj_call(x, mod, p["wbig"], p["wzs"], p["cw"], p["cb"],
                                                p["brow"], p["arow"], tb=256)
    nb = math.gcd(x.shape[0], 4)
    masks = _dn_masks()
    o_f, y_f = _mixer_call(qkv, k_t, xbc, gates, gates_t, masks, rev=False, nb=nb)
    o_dn, o_ssm = _mixer_call(qkv, k_t, xbc, gates, gates_t, masks, rev=True, nb=nb, o_fwd=o_f, y_fwd=y_f, zz=zz,
                              dn_nw=p["dn_nw"], d_row=p["d_row"], ssm_nw=p["ssm_nw"])
    return _tail_call(o_dn, o_ssm, x, mod, p["wout"], p["l1g"], p["l1b"], p["wg"], p["wu"], p["wd"],
                      p["l2g"], p["l2b"], tb=512, rows=256, alpha=alpha)


def kernel(x_prompt, x_sample, c_prompt, c_sample, w_ada, b_ada, w_in, dn_conv_w, dn_A_log, dn_dt_bias, dn_norm_w, ssm_conv_w, ssm_conv_b, ssm_A_log, ssm_dt_bias, ssm_D, ssm_norm_w, w_out, ln1_g, ln1_b, w_gate, w_up, w_down, ln2_g, ln2_b):
    depth = w_ada.shape[0]
    alpha = (2 * depth) ** 0.25
    layer_params = (w_in, dn_conv_w, dn_A_log, dn_dt_bias, dn_norm_w, ssm_conv_w, ssm_conv_b, ssm_A_log,
                    ssm_dt_bias, ssm_D, ssm_norm_w, w_out, ln1_g, ln1_b, w_gate, w_up, w_down, ln2_g, ln2_b)
    nb_p, nb_s = c_prompt.shape[0], c_sample.shape[0]
    c_all = jnp.concatenate([c_prompt, c_sample], axis=0)
    nb = nb_p + nb_s
    c_pad = jnp.pad(c_all, ((0, (-nb) % 8), (0, 0)))
    xs = [x_prompt, x_sample]
    for l in range(depth):
        p = _prep_layer(*[w[l] for w in layer_params])
        mod = _mod_call(c_pad, w_ada[l], b_ada[l].reshape(1, -1))[:nb].reshape(nb, 6, D_MODEL)
        xs = [_encoder_layer(xs[0], mod[:nb_p], p, alpha), _encoder_layer(xs[1], mod[nb_p:], p, alpha)]
    return (xs[0], xs[1])
```

```python
import functools
import math

import jax
import jax.numpy as jnp
from jax import lax
from jax.experimental import pallas as pl
from jax.experimental.pallas import tpu as pltpu

F32 = jnp.float32
BF16 = jnp.bfloat16
HIGHEST = lax.Precision.HIGHEST

D_MODEL = 1024
DN_H = 4
DN_DK = 128
DN_W = DN_H * DN_DK
SSM_H = 8
SSM_P = 64
SSM_W = SSM_H * SSM_P
SSM_G = 2
SSM_N = 128
XBC_W = SSM_W + 2 * SSM_G * SSM_N
CONV_K = 5
D_FF = 2816
CONV_W = 3 * DN_W + XBC_W
LN_EPS = 1e-5
RMS_EPS = 1e-6

CH = 128
HALO = 8
LANES = 128

G_GC = 0
G_BETA = 8
G_DT = 16
G_ACS = 32

VMEM_LIMIT = 56 * 1024 * 1024


def _sigmoid(v):
    return 1.0 / (1.0 + jnp.exp(-v))


def _silu(v):
    return v * _sigmoid(v)


def _softplus(v):
    return jnp.maximum(v, 0.0) + jnp.log1p(jnp.exp(-jnp.abs(v)))


def _dot(a, b):
    return jnp.dot(a, b, preferred_element_type=F32)


def _dot_nt(a, b):
    return lax.dot_general(a, b, (((1,), (1,)), ((), ())), preferred_element_type=F32)


def _dot_tn(a, b):
    return lax.dot_general(a, b, (((0,), (0,)), ((), ())), preferred_element_type=F32)


def _const_spec(shape):
    nd = len(shape)
    return pl.BlockSpec(shape, lambda *_: (0,) * nd, pipeline_mode=pl.Buffered(1))


def _mod_kernel(c_ref, w_ref, b_ref, o_ref):
    s = _silu(c_ref[...])
    o_ref[...] = jnp.dot(s, w_ref[...], precision=HIGHEST, preferred_element_type=F32) + b_ref[...]


def _mod_call(c_pad, w_ada, b_ada):
    bp = c_pad.shape[0]
    n = w_ada.shape[1]
    blk = 1024
    return pl.pallas_call(
        _mod_kernel,
        grid=(n // blk,),
        in_specs=[pl.BlockSpec((bp, D_MODEL), lambda j: (0, 0)),
                  pl.BlockSpec((D_MODEL, blk), lambda j: (0, j)),
                  pl.BlockSpec((1, blk), lambda j: (0, j))],
        out_specs=pl.BlockSpec((bp, blk), lambda j: (0, j)),
        out_shape=jax.ShapeDtypeStruct((bp, n), F32),
        compiler_params=pltpu.CompilerParams(dimension_semantics=("arbitrary",), vmem_limit_bytes=VMEM_LIMIT),
        name="mod",
    )(c_pad, w_ada, b_ada)


def _split3(v):
    hi = v.astype(BF16)
    r1 = v - hi.astype(F32)
    mid = r1.astype(BF16)
    lo = (r1 - mid.astype(F32)).astype(BF16)
    return hi, mid, lo


def _inproj_kernel(xm_ref, xp_ref, xn_ref, mod_ref, wbig_ref, wzs_ref, cw_ref, cb_ref, brow_ref, arow_ref,
                   qkv_ref, kt_ref, xbc_ref, zz_ref, g_ref, gt_ref, hbuf, hperm16, hm16, pbuf, ybuf, *, tb, nt):
    i = pl.program_id(1)
    sh = mod_ref[0, 0:1, :]
    sc = mod_ref[0, 1:2, :]

    def modulate(v):
        return v * (1.0 + sc) + sh

    keep_prev = jnp.where(i > 0, 1.0, 0.0)
    keep_next = jnp.where(i < nt - 1, 1.0, 0.0)
    hp = modulate(xp_ref[0]) * keep_prev
    hm = modulate(xm_ref[0])
    hn = modulate(xn_ref[0]) * keep_next
    hm16[...] = hm.astype(BF16)

    nv = (tb + 2 * HALO) // 8
    for lg in range(D_MODEL // LANES):
        ls = slice(lg * LANES, (lg + 1) * LANES)
        hbuf[lg, 0:HALO, :] = hp[:, ls]
        hbuf[lg, HALO:HALO + tb, :] = hm[:, ls]
        hbuf[lg, HALO + tb:2 * HALO + tb, :] = hn[:, ls]
    for v in range(0, nv, 2):
        hperm16[v * 8:(v + 2) * 8, :] = jnp.concatenate(
            [jnp.concatenate([hbuf[lg, pl.ds(u, 8, stride=nv), :] for lg in range(D_MODEL // LANES)], axis=1)
             for u in (v, v + 1)], axis=0).astype(BF16)

    gw = 2 * LANES
    n_groups = CONV_W // gw
    z_groups = 2 * DN_W // gw

    def project(gi):
        pbuf[:, gi * gw:(gi + 1) * gw] = _dot(hperm16[...], wbig_ref[:, gi * gw:(gi + 1) * gw])

    def tile(cs, u):
        if u < 0:
            return pltpu.roll(pbuf[(u + nv) * 8:(u + nv + 1) * 8, cs], 1, axis=0)
        if u >= nv:
            return pltpu.roll(pbuf[(u - nv) * 8:(u - nv + 1) * 8, cs], 7, axis=0)
        return pbuf[u * 8:(u + 1) * 8, cs]

    def conv(cg):
        cs = slice(cg * LANES, (cg + 1) * LANES)
        w = [cw_ref[j:j + 1, cs] for j in range(CONV_K)]
        bias = cb_ref[0:1, cs]
        for v in range(nv):
            acc = bias + tile(cs, v - CONV_K // 2) * w[0]
            for j in range(1, CONV_K):
                acc = acc + tile(cs, v + j - CONV_K // 2) * w[j]
            y = _silu(acc)
            if cg < 2 * DN_H:
                y = y * lax.rsqrt(jnp.sum(y * y, axis=-1, keepdims=True) + RMS_EPS)
                if cg < DN_H:
                    y = y * (DN_DK ** -0.5)
            ybuf[cg, pl.ds(v, 8, stride=nv), :] = y

    def emit(cg):
        y = ybuf[cg, HALO:HALO + tb, :]
        if cg < 3 * DN_H:
            qkv_ref[0, :, cg * LANES:(cg + 1) * LANES] = y.astype(BF16)
        else:
            c2 = cg - 3 * DN_H
            xbc_ref[0, :, c2 * LANES:(c2 + 1) * LANES] = y.astype(BF16)
        if DN_H <= cg < 2 * DN_H:
            for ck in range(tb // CH):
                kt_ref[0, ck, (cg - DN_H) * DN_DK:(cg - DN_H + 1) * DN_DK, :] = (
                    y[ck * CH:(ck + 1) * CH, :].T.astype(BF16))

    project(0)
    for gi in range(n_groups):
        if gi + 1 < n_groups:
            project(gi + 1)
        if gi < z_groups:
            zz_ref[0, :, gi * gw:(gi + 1) * gw] = _dot(hm16[...], wzs_ref[:, gi * gw:(gi + 1) * gw]).astype(BF16)
        for cg in range(gi * gw // LANES, (gi + 1) * gw // LANES):
            conv(cg)
            emit(cg)

    ps = _dot(hm16[...], wzs_ref[:, 2 * DN_W:2 * DN_W + LANES])
    lane = lax.broadcasted_iota(jnp.int32, (1, LANES), 1)
    sp = _softplus(ps + brow_ref[...])
    decay_src = sp * arow_ref[...]
    is_beta = (lane >= G_BETA) & (lane < G_DT)
    is_dt = (lane >= G_DT) & (lane < G_ACS)
    raw = jnp.where(is_beta, _sigmoid(ps), jnp.where(is_dt, sp, 0.0))
    fwd_cum = (lane < G_GC + DN_H) | ((lane >= G_ACS) & (lane < G_ACS + SSM_H))
    bwd_cum = ((lane >= G_GC + DN_H) & (lane < G_BETA)) | ((lane >= G_ACS + SSM_H) & (lane < G_ACS + 2 * SSM_H))
    r = lax.broadcasted_iota(jnp.int32, (CH, 2 * CH), 0)
    c = lax.broadcasted_iota(jnp.int32, (CH, 2 * CH), 1)
    tri2 = jnp.where(((c < CH) & (r >= c)) | ((c >= CH) & (r <= c - CH)), 1.0, 0.0).astype(BF16)
    for ck in range(tb // CH):
        rs = slice(ck * CH, (ck + 1) * CH)
        src = decay_src[rs]
        stacked = jnp.concatenate([jnp.where(fwd_cum, src, 0.0), jnp.where(bwd_cum, src, 0.0)], axis=0)
        hi, mid, lo = _split3(stacked)
        cum = _dot(tri2, hi) + _dot(tri2, mid) + _dot(tri2, lo)
        gates = jnp.where(fwd_cum | bwd_cum, cum, raw[rs])
        g_ref[0, rs, :] = gates
        gt_ref[0, ck] = gates.T


def _inproj_call(x, mod, wbig, wzs, cw, cb, brow, arow, tb):
    bsz, t, _ = x.shape
    nt = t // tb
    hb = tb // HALO
    nhb = t // HALO
    kern = functools.partial(_inproj_kernel, tb=tb, nt=nt)
    return pl.pallas_call(
        kern,
        grid=(bsz, nt),
        in_specs=[
            pl.BlockSpec((1, tb, D_MODEL), lambda b, i: (b, i, 0)),
            pl.BlockSpec((1, HALO, D_MODEL), lambda b, i: (b, jnp.maximum(i * hb - 1, 0), 0)),
            pl.BlockSpec((1, HALO, D_MODEL), lambda b, i: (b, jnp.minimum((i + 1) * hb, nhb - 1), 0)),
            pl.BlockSpec((1, 6, D_MODEL), lambda b, i: (b, 0, 0)),
            _const_spec((D_MODEL, CONV_W)),
            _const_spec((D_MODEL, 2 * DN_W + LANES)),
            _const_spec((CONV_K, CONV_W)),
            _const_spec((1, CONV_W)),
            _const_spec((1, LANES)),
            _const_spec((1, LANES)),
        ],
        out_specs=[
            pl.BlockSpec((1, tb, 3 * DN_W), lambda b, i: (b, i, 0)),
            pl.BlockSpec((1, tb // CH, DN_W, CH), lambda b, i: (b, i, 0, 0)),
            pl.BlockSpec((1, tb, XBC_W), lambda b, i: (b, i, 0)),
            pl.BlockSpec((1, tb, 2 * DN_W), lambda b, i: (b, i, 0)),
            pl.BlockSpec((1, tb, LANES), lambda b, i: (b, i, 0)),
            pl.BlockSpec((1, tb // CH, LANES, CH), lambda b, i: (b, i, 0, 0)),
        ],
        out_shape=[
            jax.ShapeDtypeStruct((bsz, t, 3 * DN_W), BF16),
            jax.ShapeDtypeStruct((bsz, t // CH, DN_W, CH), BF16),
            jax.ShapeDtypeStruct((bsz, t, XBC_W), BF16),
            jax.ShapeDtypeStruct((bsz, t, 2 * DN_W), BF16),
            jax.ShapeDtypeStruct((bsz, t, LANES), F32),
            jax.ShapeDtypeStruct((bsz, t // CH, LANES, CH), F32),
        ],
        scratch_shapes=[pltpu.VMEM((D_MODEL // LANES, tb + 2 * HALO, LANES), F32),
                        pltpu.VMEM((tb + 2 * HALO, D_MODEL), BF16),
                        pltpu.VMEM((tb, D_MODEL), BF16),
                        pltpu.VMEM((tb + 2 * HALO, CONV_W), F32),
                        pltpu.VMEM((CONV_W // LANES, tb + 2 * HALO, LANES), F32)],
        compiler_params=pltpu.CompilerParams(dimension_semantics=("arbitrary", "arbitrary"),
                                             vmem_limit_bytes=VMEM_LIMIT),
        name="inproj",
    )(x, x, x, mod, wbig, wzs, cw, cb, brow, arow)


def _dn_masks():
    r = jnp.arange(CH)[:, None]
    c = jnp.arange(CH)[None, :]
    ms = [(r >> 1) == (c >> 1)]
    for s in range(1, 7):
        ms.append(((r >> (s + 1)) == (c >> (s + 1))) & ((r >> s) != (c >> s)))
    ms.append(r == c)
    return jnp.stack(ms).astype(BF16)


def _dn_stages(refs, rev, nb):
    qkv_ref, kt_ref, g_ref, gt_ref, m_ref, o_ref, s_ref, gb_s, bb_s, l16, x16, t16, a16 = refs

    r = lax.broadcasted_iota(jnp.int32, (CH, CH), 0)
    c = lax.broadcasted_iota(jnp.int32, (CH, CH), 1)
    incl = (r <= c) if rev else (r >= c)
    strict = (r < c) if rev else (r > c)
    lo = DN_H if rev else 0
    last = 0 if rev else CH - 1
    chains = [(bi, h) for bi in range(nb) for h in range(DN_H)]
    n = len(chains)

    def q_of(bi, h):
        return qkv_ref[bi, :, h * DN_DK:(h + 1) * DN_DK]

    def k_of(bi, h):
        return qkv_ref[bi, :, DN_W + h * DN_DK:DN_W + (h + 1) * DN_DK]

    def v_of(bi, h):
        return qkv_ref[bi, :, 2 * DN_W + h * DN_DK:2 * DN_W + (h + 1) * DN_DK]

    def kt_of(bi, h):
        return kt_ref[bi, 0, h * DN_DK:(h + 1) * DN_DK, :]

    def gcc_of(bi, h):
        return g_ref[bi, :, G_GC + lo + h:G_GC + lo + h + 1]

    def beta_of(bi, h):
        return g_ref[bi, :, G_BETA + lo + h:G_BETA + lo + h + 1]

    def stage_a():
        for ci, (bi, h) in enumerate(chains):
            gi = G_GC + lo + h
            gb = jnp.broadcast_to(gcc_of(bi, h), (CH, CH))
            bb = jnp.broadcast_to(beta_of(bi, h), (CH, CH))
            gb_s[ci] = gb
            bb_s[ci] = bb
            decay = jnp.exp(jnp.where(incl, gb - gt_ref[bi, 0, gi:gi + 1, :], -jnp.inf))
            kt16 = kt_of(bi, h)
            lm = jnp.where(strict, _dot((k_of(bi, h) * bb).astype(BF16), kt16) * decay, 0.0).astype(BF16)
            l16[ci] = lm
            x16[ci] = m_ref[7] - lm * m_ref[0]
            a16[ci] = (_dot(q_of(bi, h).astype(BF16), kt16) * decay).astype(BF16)

    def level(s):
        for ci in range(n):
            t16[ci] = _dot(l16[ci] * m_ref[s], x16[ci]).astype(BF16)
        for ci in range(n):
            x16[ci] = x16[ci] - _dot(x16[ci], t16[ci]).astype(BF16)

    def stage_c():
        for ci, (bi, h) in enumerate(chains):
            kg16 = (k_of(bi, h) * jnp.exp(gb_s[ci])).astype(BF16)
            resid = v_of(bi, h) - _dot(kg16, s_ref[ci].astype(BF16))
            t16[ci] = (bb_s[ci] * resid).astype(BF16)
        for ci in range(n):
            l16[ci] = _dot(x16[ci], t16[ci]).astype(BF16)
        for ci, (bi, h) in enumerate(chains):
            hs = slice(h * DN_DK, (h + 1) * DN_DK)
            gi = G_GC + lo + h
            glast = g_ref[bi, last:last + 1, gi:gi + 1]
            st = s_ref[ci]
            vnew16 = l16[ci]
            qg16 = (q_of(bi, h) * jnp.exp(gb_s[ci])).astype(BF16)
            o = _dot(jnp.concatenate([qg16, a16[ci]], axis=1), jnp.concatenate([st.astype(BF16), vnew16], axis=0))
            kdec_t16 = (kt_of(bi, h) * jnp.exp(glast - gt_ref[bi, 0, gi:gi + 1, :])).astype(BF16)
            s_ref[ci] = st * jnp.exp(glast) + _dot(kdec_t16, vnew16)
            o_ref[bi, :, hs] = o.astype(o_ref.dtype)

    return stage_a, level, stage_c


def _ssd_stages(refs, rev, nb):
    xbc_ref, g_ref, gt_ref, y_ref, s_ref = refs

    r = lax.broadcasted_iota(jnp.int32, (CH, CH), 0)
    c = lax.broadcasted_iota(jnp.int32, (CH, CH), 1)
    incl = (r <= c) if rev else (r >= c)
    left = lax.broadcasted_iota(jnp.int32, (1, LANES), 1) < SSM_P
    lo = SSM_H if rev else 0
    last = 0 if rev else CH - 1
    pairs_per_group = SSM_H // SSM_G // 2
    n_pairs = SSM_H // 2

    def b_of(bi, g):
        return xbc_ref[bi, :, SSM_W + g * SSM_N:SSM_W + (g + 1) * SSM_N]

    def c16_of(bi, g):
        o = SSM_W + SSM_G * SSM_N
        return xbc_ref[bi, :, o + g * SSM_N:o + (g + 1) * SSM_N].astype(BF16)

    bts, cbs, yoffs = {}, {}, {}

    def stage1():
        for bi in range(nb):
            for g in range(SSM_G):
                c16 = c16_of(bi, g)
                bts[bi, g] = b_of(bi, g).astype(F32).T
                cbs[bi, g] = _dot(c16, bts[bi, g].astype(BF16))
                for pp in range(pairs_per_group):
                    p = g * pairs_per_group + pp
                    yoffs[bi, p] = _dot(c16, s_ref[bi * n_pairs + p].astype(BF16))

    def group(bi, g):
        bt = bts[bi, g]
        cb = cbs[bi, g]
        for pp in range(pairs_per_group):
            p = g * pairs_per_group + pp
            ps = slice(p * LANES, (p + 1) * LANES)
            x = xbc_ref[bi, :, ps].astype(F32)
            d0 = G_DT + lo + 2 * p
            a0 = G_ACS + lo + 2 * p
            ac0b = jnp.broadcast_to(g_ref[bi, :, a0:a0 + 1], (CH, LANES))
            ac1b = jnp.broadcast_to(g_ref[bi, :, a0 + 1:a0 + 2], (CH, LANES))
            ar0, ar1 = gt_ref[bi, 0, a0:a0 + 1, :], gt_ref[bi, 0, a0 + 1:a0 + 2, :]
            dr0, dr1 = gt_ref[bi, 0, d0:d0 + 1, :], gt_ref[bi, 0, d0 + 1:d0 + 2, :]
            al0 = g_ref[bi, last:last + 1, a0:a0 + 1]
            al1 = g_ref[bi, last:last + 1, a0 + 1:a0 + 2]
            l0 = jnp.exp(jnp.where(incl, ac0b - ar0, -jnp.inf)) * dr0
            l1 = jnp.exp(jnp.where(incl, ac1b - ar1, -jnp.inf)) * dr1
            w0 = dr0 * jnp.exp(al0 - ar0)
            w1 = dr1 * jnp.exp(al1 - ar1)
            lhs = jnp.concatenate(
                [jnp.concatenate([(cb * l0).astype(BF16), (cb * l1).astype(BF16)], axis=1),
                 jnp.concatenate([(bt * w0).astype(BF16), (bt * w1).astype(BF16)], axis=1)], axis=0)
            x01 = jnp.concatenate([jnp.where(left, x, 0.0).astype(BF16),
                                   jnp.where(left, 0.0, x).astype(BF16)], axis=0)
            prod = _dot(lhs, x01)
            acp = jnp.where(left, ac0b, ac1b)
            alp = jnp.where(left, al0, al1)
            y = prod[0:CH] + yoffs[bi, p] * jnp.exp(acp)
            si = bi * n_pairs + p
            s_ref[si] = s_ref[si] * jnp.exp(alp) + prod[CH:2 * CH]
            y_ref[bi, :, ps] = y.astype(y_ref.dtype)

    return stage1, group


def _mixer_kernel(qkv_ref, kt_ref, xbc_ref, g_ref, gt_ref, m_ref, o_ref, y_ref,
                  sd_ref, gb_s, bb_s, l16, x16, t16, a16, ss_ref, *, rev, nb):
    dn_refs = (qkv_ref, kt_ref, g_ref, gt_ref, m_ref, o_ref, sd_ref, gb_s, bb_s, l16, x16, t16, a16)
    ssd_refs = (xbc_ref, g_ref, gt_ref, y_ref, ss_ref)

    @pl.when(pl.program_id(1) == 0)
    def _():
        sd_ref[...] = jnp.zeros_like(sd_ref)
        ss_ref[...] = jnp.zeros_like(ss_ref)

    dn_a, dn_level, dn_c = _dn_stages(dn_refs, rev, nb)
    ssd_1, ssd_group = _ssd_stages(ssd_refs, rev, nb)
    groups = [(bi, g) for bi in range(nb) for g in range(SSM_G)]
    n_levels = 6
    dn_a()
    ssd_1()
    done = 0
    for s in range(1, n_levels + 1):
        dn_level(s)
        upto = len(groups) * s // n_levels
        for bi, g in groups[done:upto]:
            ssd_group(bi, g)
        done = upto
    dn_c()


def _mixer_call(qkv, k_t, xbc, gates, gates_t, masks, rev, nb):
    bsz, t, _ = qkv.shape
    nc = t // CH

    def cidx(i):
        return (nc - 1 - i) if rev else i

    def tok(width, col=0):
        return pl.BlockSpec((nb, CH, width), lambda b, i: (b, cidx(i), col))

    def per_chunk(rows):
        return pl.BlockSpec((nb, 1, rows, CH), lambda b, i: (b, cidx(i), 0, 0))

    in_specs = [tok(3 * DN_W), per_chunk(DN_W), tok(XBC_W), tok(LANES), per_chunk(LANES), _const_spec((8, CH, CH))]
    args = [qkv, k_t, xbc, gates, gates_t, masks]
    n = nb * DN_H
    return pl.pallas_call(
        functools.partial(_mixer_kernel, rev=rev, nb=nb),
        grid=(bsz // nb, nc),
        in_specs=in_specs,
        out_specs=[tok(DN_W), tok(SSM_W)],
        out_shape=[jax.ShapeDtypeStruct((bsz, t, DN_W), BF16), jax.ShapeDtypeStruct((bsz, t, SSM_W), BF16)],
        scratch_shapes=[pltpu.VMEM((n, DN_DK, DN_DK), F32)] + [pltpu.VMEM((n, CH, CH), F32)] * 2
        + [pltpu.VMEM((n, CH, CH), BF16)] * 4 + [pltpu.VMEM((nb * SSM_H // 2, SSM_N, 2 * SSM_P), F32)],
        compiler_params=pltpu.CompilerParams(dimension_semantics=("arbitrary", "arbitrary"),
                                             vmem_limit_bytes=VMEM_LIMIT),
        name="mixer_bwd" if rev else "mixer_fwd",
    )(*args)


def _layer_norm(v, g, b):
    mu = jnp.mean(v, axis=-1, keepdims=True)
    d = v - mu
    var = jnp.mean(d * d, axis=-1, keepdims=True)
    return d * lax.rsqrt(var + LN_EPS) * g + b


def _tail_kernel(of_ref, ob_ref, yf_ref, yb_ref, zz_ref, xs_ref, x_ref, mod_ref, dnw_ref, drow_ref, snw_ref,
                 wout_ref, l1g_ref, l1b_ref, wg_ref, wu_ref, wd_ref, l2g_ref, l2b_ref, out_ref, *, alpha, rows):
    gt1 = mod_ref[0, 2:3, :]
    sh2 = mod_ref[0, 3:4, :]
    sc2 = mod_ref[0, 4:5, :]
    gt2 = mod_ref[0, 5:6, :]

    def mixer_out(rs):
        cols = []
        for h in range(DN_H):
            hs = slice(h * DN_DK, (h + 1) * DN_DK)
            o = of_ref[0, rs, hs].astype(F32) + ob_ref[0, rs, hs].astype(F32)
            o = o * lax.rsqrt(jnp.mean(o * o, axis=-1, keepdims=True) + RMS_EPS)
            cols.append((o * dnw_ref[...] * _silu(zz_ref[0, rs, hs].astype(F32))).astype(BF16))
        gw = SSM_W // SSM_G
        for g in range(SSM_G):
            gs = slice(g * gw, (g + 1) * gw)
            y = (yf_ref[0, rs, gs].astype(F32) + yb_ref[0, rs, gs].astype(F32)
                 + drow_ref[0:1, gs] * xs_ref[0, rs, gs].astype(F32))
            y = y * _silu(zz_ref[0, rs, DN_W + g * gw:DN_W + (g + 1) * gw].astype(F32))
            y = y * lax.rsqrt(jnp.mean(y * y, axis=-1, keepdims=True) + RMS_EPS)
            cols.append((y * snw_ref[0:1, gs]).astype(BF16))
        return jnp.concatenate(cols, axis=1)

    bounds = [0]
    for r in rows:
        bounds.append(bounds[-1] + r)
    parts = [slice(a, b) for a, b in zip(bounds[:-1], bounds[1:])]

    def front(rs):
        mix = _dot(mixer_out(rs), wout_ref[...])
        x1 = _layer_norm(alpha * x_ref[0, rs, :] + gt1 * mix, l1g_ref[...], l1b_ref[...])
        return x1, (x1 * (1.0 + sc2) + sh2).astype(BF16)

    def gate_up(h16):
        return _dot(h16, wg_ref[...]), _dot(h16, wu_ref[...])

    def finish(rs, x1, ffn):
        out_ref[0, rs, :] = _layer_norm(alpha * x1 + gt2 * ffn, l2g_ref[...], l2b_ref[...])

    fronts = [front(rs) for rs in parts]
    gus = [gate_up(h16) for _, h16 in fronts]
    acts = [(_silu(g) * u).astype(BF16) for g, u in gus]
    ffns = [_dot(a, wd_ref[...]) for a in acts]
    for rs, (x1, _), ffn in zip(parts, fronts, ffns):
        finish(rs, x1, ffn)


def _tail_call(o_f, o_b, y_f, y_b, zz, xbc, x, mod, dn_nw, d_row, ssm_nw, wout, l1g, l1b, wg, wu, wd, l2g, l2b,
               tb, rows, alpha):
    bsz, t, _ = x.shape
    nt = t // tb

    def tok(width):
        return pl.BlockSpec((1, tb, width), lambda b, i: (b, i, 0))

    return pl.pallas_call(
        functools.partial(_tail_kernel, alpha=alpha, rows=rows),
        grid=(bsz, nt),
        in_specs=[
            tok(DN_W), tok(DN_W), tok(SSM_W), tok(SSM_W),
            tok(2 * DN_W),
            tok(SSM_W),
            tok(D_MODEL),
            pl.BlockSpec((1, 6, D_MODEL), lambda b, i: (b, 0, 0)),
            _const_spec((1, DN_DK)),
            _const_spec((1, SSM_W)),
            _const_spec((1, SSM_W)),
            _const_spec((DN_W + SSM_W, D_MODEL)),
            _const_spec((1, D_MODEL)),
            _const_spec((1, D_MODEL)),
            _const_spec((D_MODEL, D_FF)),
            _const_spec((D_MODEL, D_FF)),
            _const_spec((D_FF, D_MODEL)),
            _const_spec((1, D_MODEL)),
            _const_spec((1, D_MODEL)),
        ],
        out_specs=pl.BlockSpec((1, tb, D_MODEL), lambda b, i: (b, i, 0)),
        out_shape=jax.ShapeDtypeStruct((bsz, t, D_MODEL), F32),
        compiler_params=pltpu.CompilerParams(dimension_semantics=("arbitrary", "arbitrary"),
                                             vmem_limit_bytes=VMEM_LIMIT),
        name="tail",
    )(o_f, o_b, y_f, y_b, zz, xbc, x, mod, dn_nw, d_row, ssm_nw, wout, l1g, l1b, wg, wu, wd, l2g, l2b)


def _prep_layer(w_in, dn_conv_w, dn_A_log, dn_dt_bias, dn_norm_w, ssm_conv_w, ssm_conv_b, ssm_A_log,
                ssm_dt_bias, ssm_D, ssm_norm_w, w_out, ln1_g, ln1_b, w_gate, w_up, w_down, ln2_g, ln2_b):
    o = 0
    w_qkv = w_in[:, o:o + 3 * DN_W]; o += 3 * DN_W
    w_dz = w_in[:, o:o + DN_W]; o += DN_W
    w_a = w_in[:, o:o + 2 * DN_H]; o += 2 * DN_H
    w_b = w_in[:, o:o + 2 * DN_H]; o += 2 * DN_H
    w_sz = w_in[:, o:o + SSM_W]; o += SSM_W
    w_xbc = w_in[:, o:o + XBC_W]; o += XBC_W
    w_dt = w_in[:, o:o + 2 * SSM_H]
    wbig = jnp.concatenate([w_qkv, w_xbc], axis=1).astype(BF16)
    pad = jnp.zeros((D_MODEL, LANES - G_ACS - 2 * SSM_H), F32)
    wzs = jnp.concatenate([w_dz, w_sz, w_a, w_b, w_dt, w_dt, pad], axis=1).astype(BF16)
    zrow = lambda n: jnp.zeros((n,), F32)
    brow = jnp.concatenate([dn_dt_bias.reshape(-1), zrow(2 * DN_H), ssm_dt_bias.reshape(-1),
                            ssm_dt_bias.reshape(-1), zrow(LANES - G_ACS - 2 * SSM_H)]).reshape(1, LANES)
    arow = jnp.concatenate([-jnp.exp(dn_A_log.reshape(-1)), zrow(2 * DN_H + 2 * SSM_H),
                            -jnp.exp(ssm_A_log.reshape(-1)), zrow(LANES - G_ACS - 2 * SSM_H)]).reshape(1, LANES)
    cw = jnp.concatenate([dn_conv_w, ssm_conv_w], axis=1)
    cb = jnp.concatenate([zrow(3 * DN_W), ssm_conv_b]).reshape(1, CONV_W)
    return dict(
        wbig=wbig, wzs=wzs, brow=brow, arow=arow, cw=cw, cb=cb,
        dn_nw=dn_norm_w.reshape(1, DN_DK), d_row=jnp.repeat(ssm_D, SSM_P).reshape(1, SSM_W),
        ssm_nw=ssm_norm_w.reshape(1, SSM_W), wout=w_out.astype(BF16),
        l1g=ln1_g.reshape(1, D_MODEL), l1b=ln1_b.reshape(1, D_MODEL),
        wg=w_gate.astype(BF16), wu=w_up.astype(BF16), wd=w_down.astype(BF16),
        l2g=ln2_g.reshape(1, D_MODEL), l2b=ln2_b.reshape(1, D_MODEL))


def _encoder_layer(x, mod, p, alpha):
    qkv, k_t, xbc, zz, gates, gates_t = _inproj_call(x, mod, p["wbig"], p["wzs"], p["cw"], p["cb"],
                                                p["brow"], p["arow"], tb=512)
    nb = math.gcd(x.shape[0], 4)
    masks = _dn_masks()
    o_f, y_f = _mixer_call(qkv, k_t, xbc, gates, gates_t, masks, rev=False, nb=nb)
    o_b, y_b = _mixer_call(qkv, k_t, xbc, gates, gates_t, masks, rev=True, nb=nb)
    return _tail_call(o_f, o_b, y_f, y_b, zz, xbc, x, mod, p["dn_nw"], p["d_row"], p["ssm_nw"], p["wout"],
                      p["l1g"], p["l1b"], p["wg"], p["wu"], p["wd"], p["l2g"], p["l2b"],
                      tb=512, rows=(256, 256), alpha=alpha)


def kernel(x_prompt, x_sample, c_prompt, c_sample, w_ada, b_ada, w_in, dn_conv_w, dn_A_log, dn_dt_bias, dn_norm_w, ssm_conv_w, ssm_conv_b, ssm_A_log, ssm_dt_bias, ssm_D, ssm_norm_w, w_out, ln1_g, ln1_b, w_gate, w_up, w_down, ln2_g, ln2_b):
    depth = w_ada.shape[0]
    alpha = (2 * depth) ** 0.25
    layer_params = (w_in, dn_conv_w, dn_A_log, dn_dt_bias, dn_norm_w, ssm_conv_w, ssm_conv_b, ssm_A_log,
                    ssm_dt_bias, ssm_D, ssm_norm_w, w_out, ln1_g, ln1_b, w_gate, w_up, w_down, ln2_g, ln2_b)
    nb_p, nb_s = c_prompt.shape[0], c_sample.shape[0]
    c_all = jnp.concatenate([c_prompt, c_sample], axis=0)
    nb = nb_p + nb_s
    c_pad = jnp.pad(c_all, ((0, (-nb) % 8), (0, 0)))
    xs = [x_prompt, x_sample]
    for l in range(depth):
        p = _prep_layer(*[w[l] for w in layer_params])
        mod = _mod_call(c_pad, w_ada[l], b_ada[l].reshape(1, -1))[:nb].reshape(nb, 6, D_MODEL)
        xs = [_encoder_layer(xs[0], mod[:nb_p], p, alpha), _encoder_layer(xs[1], mod[nb_p:], p, alpha)]
    return (xs[0], xs[1])
```
